```python
import math
import jax, jax.numpy as jnp
from jax import lax
import numpy as np

D_MODEL = 1024
BATCH = 8
SEQ = 2048
DEPTH = 4
DEC_BATCH = 32
DEC_SEQ = 4
PAST_LEN = 8192
PAGE_SIZE = 128

N_MIXERS = 2
N_REC_LAYERS = (DEPTH + N_MIXERS - 1) // N_MIXERS
N_ATTN_LAYERS = DEPTH // N_MIXERS
D_RNN = D_MODEL
N_RNN_BLOCKS = 4
RNN_BLOCK = D_RNN // N_RNN_BLOCKS
CONV_W = 4
LRU_C = 8.0
HEAD_DIM = 64
N_HEADS = D_MODEL // (2 * HEAD_DIM)
V_DIM = 2 * HEAD_DIM
ROT_DIM = HEAD_DIM // 4
ROPE_THETA = 500000.0
D_FF = -(-8 * D_MODEL // (3 * 256)) * 256
Q_BLOCK = 128
NORM_EPS = 1e-6
SUBLN_EPS = 1e-5
NEG_INF = -1e30

kernel_name = "hybrid_rglru_diffattn_decode_step"


def rms_norm(x, g, eps=NORM_EPS):
    xf = x.astype(jnp.float32)
    y = xf * lax.rsqrt(jnp.mean(xf * xf, axis=-1, keepdims=True) + eps)
    return (y * g.astype(jnp.float32)).astype(x.dtype)


def rope(x, pos):
    half = ROT_DIM // 2
    inv = jnp.power(ROPE_THETA, -jnp.arange(half, dtype=jnp.float32) * 2.0 / ROT_DIM)
    ang = pos[:, None] * inv[None, :]
    cos = jnp.cos(ang)[None, :, None, None, :].astype(x.dtype)
    sin = jnp.sin(ang)[None, :, None, None, :].astype(x.dtype)
    x1 = x[..., :half]
    x2 = x[..., half:ROT_DIM]
    return jnp.concatenate([x1 * cos - x2 * sin, x2 * cos + x1 * sin, x[..., ROT_DIM:]], axis=-1)


def causal_conv(x, buf, w, b):
    T = x.shape[1]
    xe = jnp.concatenate([buf.astype(x.dtype), x], axis=1)
    y = b
    for k in range(CONV_W):
        y = y + xe[:, k:k + T] * w[k]
    return y, xe[:, -(CONV_W - 1):]


def rg_lru(x, h0, wx, bx, wa, ba, lam):
    B, T, _ = x.shape
    xb = x.reshape(B, T, N_RNN_BLOCKS, RNN_BLOCK)
    gx = jax.nn.sigmoid((jnp.einsum('btnd,nde->btne', xb, wx).reshape(B, T, D_RNN) + bx).astype(jnp.float32))
    ga = jax.nn.sigmoid((jnp.einsum('btnd,nde->btne', xb, wa).reshape(B, T, D_RNN) + ba).astype(jnp.float32))
    log_a = LRU_C * ga * jax.nn.log_sigmoid(lam.astype(jnp.float32))
    a = jnp.exp(log_a)
    mult = jnp.sqrt(jnp.maximum(-jnp.expm1(2.0 * log_a), 0.0))
    u = mult * gx * x.astype(jnp.float32)

    def step(h, au):
        a_t, u_t = au
        h = a_t * h + u_t
        return h, h

    hT, hs = lax.scan(step, h0.astype(jnp.float32), (jnp.swapaxes(a, 0, 1), jnp.swapaxes(u, 0, 1)))
    return jnp.swapaxes(hs, 0, 1).astype(x.dtype), hT.astype(x.dtype)


def recurrent_block(x, conv_buf, h0, w_in, conv_w, conv_b, wx, bx, wa, ba, lam, w_out):
    xr, gate = jnp.split(x @ w_in, 2, axis=-1)
    xc, new_buf = causal_conv(xr, conv_buf, conv_w, conv_b)
    hs, hT = rg_lru(xc, h0, wx, bx, wa, ba, lam)
    y = (hs * jax.nn.gelu(gate)) @ w_out
    return y, new_buf, hT


def qkv_proj(x, w_qkv, pos):
    B, T, _ = x.shape
    q, k, v = jnp.split(x @ w_qkv, 3, axis=-1)
    q = rope(q.reshape(B, T, N_HEADS, 2, HEAD_DIM), pos)
    k = rope(k.reshape(B, T, N_HEADS, 2, HEAD_DIM), pos)
    v = v.reshape(B, T, N_HEADS, V_DIM)
    return q, k, v


def diff_lambda(lq1, lk1, lq2, lk2, lam_init):
    f = lambda t: t.astype(jnp.float32)
    return jnp.exp(jnp.sum(f(lq1) * f(lk1))) - jnp.exp(jnp.sum(f(lq2) * f(lk2))) + lam_init


def diff_core(q, k, v, q_pos, k_pos, lam):
    s = jnp.einsum('bqhcd,bkhcd->bhcqk', q, k).astype(jnp.float32) * (HEAD_DIM ** -0.5)
    mask = k_pos[None, :] <= q_pos[:, None]
    s = jnp.where(mask, s, NEG_INF)
    p = jax.nn.softmax(s, axis=-1)
    w = p[:, :, 0] - lam * p[:, :, 1]
    return jnp.einsum('bhqk,bkhe->bqhe', w.astype(v.dtype), v)


def prompt_attention(q, k, v, lam):
    B, T = q.shape[:2]
    nb = T // Q_BLOCK
    qb = jnp.moveaxis(q.reshape(B, nb, Q_BLOCK, N_HEADS, 2, HEAD_DIM), 1, 0)
    k_pos = jnp.arange(T)

    def blk(args):
        qi, i = args
        q_pos = i * Q_BLOCK + jnp.arange(Q_BLOCK)
        return diff_core(qi, k, v, q_pos, k_pos, lam)

    o = lax.map(blk, (qb, jnp.arange(nb)))
    return jnp.moveaxis(o, 0, 1).reshape(B, T, N_HEADS, V_DIM)


def sample_attention(q, k_new, v_new, cache_k_l, cache_v_l, page_table, lam):
    DB, Tn = q.shape[:2]
    past = page_table.shape[1] * PAGE_SIZE
    k_past = cache_k_l[page_table].reshape(DB, past, N_HEADS, 2, HEAD_DIM)
    v_past = cache_v_l[page_table].reshape(DB, past, N_HEADS, V_DIM)
    k = jnp.concatenate([k_past.astype(k_new.dtype), k_new], axis=1)
    v = jnp.concatenate([v_past.astype(v_new.dtype), v_new], axis=1)
    q_pos = past + jnp.arange(Tn)
    k_pos = jnp.arange(past + Tn)
    return diff_core(q, k, v, q_pos, k_pos, lam)


def diff_out(o, lam_init, subln_g, w_o):
    B, T = o.shape[:2]
    o = rms_norm(o, subln_g, SUBLN_EPS) * (1.0 - lam_init)
    return o.reshape(B, T, D_MODEL) @ w_o


def swiglu(x, w_in, w_out):
    g, u = jnp.split(x @ w_in, 2, axis=-1)
    return (jax.nn.silu(g) * u) @ w_out


def setup_inputs(seed: int = 0) -> dict:
    key = jax.random.key(seed)
    ks = iter(jax.random.split(key, 40))
    f32 = jnp.float32
    n_pages = PAST_LEN // PAGE_SIZE
    n_used = DEC_BATCH * n_pages
    n_pool = n_used + n_used // 4
    nrm = lambda shape, s=1.0: jax.random.normal(next(ks), shape, f32) * s
    page_table = jax.random.permutation(next(ks), n_pool)[:n_used].reshape(DEC_BATCH, n_pages).astype(jnp.int32)
    u = jax.random.uniform(next(ks), (N_REC_LAYERS, D_RNN), f32, 0.9, 0.999)
    return {
        "x_prompt": nrm((BATCH, SEQ, D_MODEL)),
        "x_sample": nrm((DEC_BATCH, DEC_SEQ, D_MODEL)),
        "cache_k": nrm((N_ATTN_LAYERS, n_pool, PAGE_SIZE, N_HEADS, 2, HEAD_DIM)),
        "cache_v": nrm((N_ATTN_LAYERS, n_pool, PAGE_SIZE, N_HEADS, V_DIM)),
        "page_table": page_table,
        "state_conv": nrm((N_REC_LAYERS, DEC_BATCH, CONV_W - 1, D_RNN)),
        "state_h": nrm((N_REC_LAYERS, DEC_BATCH, D_RNN), 0.5),
        "norm_mix": 1.0 + nrm((DEPTH, D_MODEL), 0.02),
        "w_rec_in": nrm((N_REC_LAYERS, D_MODEL, 2 * D_RNN), D_MODEL ** -0.5),
        "conv_w": nrm((N_REC_LAYERS, CONV_W, D_RNN), CONV_W ** -0.5),
        "conv_b": nrm((N_REC_LAYERS, D_RNN), 0.02),
        "gate_x_w": nrm((N_REC_LAYERS, N_RNN_BLOCKS, RNN_BLOCK, RNN_BLOCK), RNN_BLOCK ** -0.5),
        "gate_x_b": nrm((N_REC_LAYERS, D_RNN), 0.02),
        "gate_a_w": nrm((N_REC_LAYERS, N_RNN_BLOCKS, RNN_BLOCK, RNN_BLOCK), RNN_BLOCK ** -0.5),
        "gate_a_b": nrm((N_REC_LAYERS, D_RNN), 0.02),
        "lru_lambda": jnp.log(u) - jnp.log1p(-u),
        "w_rec_out": nrm((N_REC_LAYERS, D_RNN, D_MODEL), D_RNN ** -0.5),
        "w_qkv": nrm((N_ATTN_LAYERS, D_MODEL, 3 * D_MODEL), D_MODEL ** -0.5),
        "lambda_q1": nrm((N_ATTN_LAYERS, HEAD_DIM), 0.1),
        "lambda_k1": nrm((N_ATTN_LAYERS, HEAD_DIM), 0.1),
        "lambda_q2": nrm((N_ATTN_LAYERS, HEAD_DIM), 0.1),
        "lambda_k2": nrm((N_ATTN_LAYERS, HEAD_DIM), 0.1),
        "subln_g": 1.0 + nrm((N_ATTN_LAYERS, V_DIM), 0.02),
        "w_attn_out": nrm((N_ATTN_LAYERS, D_MODEL, D_MODEL), D_MODEL ** -0.5),
        "norm_ffn": 1.0 + nrm((DEPTH, D_MODEL), 0.02),
        "w_ffn_in": nrm((DEPTH, D_MODEL, 2 * D_FF), D_MODEL ** -0.5),
        "w_ffn_out": nrm((DEPTH, D_FF, D_MODEL), D_FF ** -0.5),
        "norm_final": 1.0 + nrm((D_MODEL,), 0.02),
    }


def reference(x_prompt, x_sample, cache_k, cache_v, page_table, state_conv, state_h,
              norm_mix, w_rec_in, conv_w, conv_b, gate_x_w, gate_x_b, gate_a_w, gate_a_b,
              lru_lambda, w_rec_out, w_qkv, lambda_q1, lambda_k1, lambda_q2, lambda_k2,
              subln_g, w_attn_out, norm_ffn, w_ffn_in, w_ffn_out, norm_final):
    B, T, _ = x_prompt.shape
    DB, Tn, _ = x_sample.shape
    past = page_table.shape[1] * PAGE_SIZE
    pos_p = jnp.arange(T, dtype=jnp.float32)
    pos_s = past + jnp.arange(Tn, dtype=jnp.float32)
    zero_conv = jnp.zeros((B, CONV_W - 1, D_RNN), x_prompt.dtype)
    zero_h = jnp.zeros((B, D_RNN), x_prompt.dtype)

    h_p, h_s = x_prompt, x_sample
    pk, pv, pconv, ph = [], [], [], []
    sk, sv, sconv, sh = [], [], [], []
    for i in range(DEPTH):
        n_p = rms_norm(h_p, norm_mix[i])
        n_s = rms_norm(h_s, norm_mix[i])
        if i % N_MIXERS == 0:
            r = i // N_MIXERS
            w = (w_rec_in[r], conv_w[r], conv_b[r], gate_x_w[r], gate_x_b[r],
                 gate_a_w[r], gate_a_b[r], lru_lambda[r], w_rec_out[r])
            y_p, c_p, hh_p = recurrent_block(n_p, zero_conv, zero_h, *w)
            y_s, c_s, hh_s = recurrent_block(n_s, state_conv[r], state_h[r], *w)
            pconv.append(c_p); ph.append(hh_p)
            sconv.append(c_s); sh.append(hh_s)
        else:
            a = i // N_MIXERS
            lam_init = 0.8 - 0.6 * math.exp(-0.3 * i)
            lam = diff_lambda(lambda_q1[a], lambda_k1[a], lambda_q2[a], lambda_k2[a], lam_init)
            q_p, k_p, v_p = qkv_proj(n_p, w_qkv[a], pos_p)
            y_p = diff_out(prompt_attention(q_p, k_p, v_p, lam), lam_init, subln_g[a], w_attn_out[a])
            q_s, k_s, v_s = qkv_proj(n_s, w_qkv[a], pos_s)
            o_s = sample_attention(q_s, k_s, v_s, cache_k[a], cache_v[a], page_table, lam)
            y_s = diff_out(o_s, lam_init, subln_g[a], w_attn_out[a])
            pk.append(k_p); pv.append(v_p)
            sk.append(k_s); sv.append(v_s)
        h_p = h_p + y_p
        h_s = h_s + y_s
        h_p = h_p + swiglu(rms_norm(h_p, norm_ffn[i]), w_ffn_in[i], w_ffn_out[i])
        h_s = h_s + swiglu(rms_norm(h_s, norm_ffn[i]), w_ffn_in[i], w_ffn_out[i])

    y_prompt = rms_norm(h_p, norm_final)
    y_sample = rms_norm(h_s, norm_final)
    return (y_prompt, y_sample,
            jnp.stack(pk), jnp.stack(pv), jnp.stack(pconv), jnp.stack(ph),
            jnp.stack(sk), jnp.stack(sv), jnp.stack(sconv), jnp.stack(sh))
```

```python
import functools
import math

import jax
import jax.numpy as jnp
from jax import lax
from jax.experimental import pallas as pl
from jax.experimental.pallas import tpu as pltpu

F32 = jnp.float32
BF16 = jnp.bfloat16

D_MODEL = 1024
DEPTH = 4
N_MIXERS = 2
N_ATTN_LAYERS = DEPTH // N_MIXERS
D_RNN = D_MODEL
N_RNN_BLOCKS = 4
RNN_BLOCK = D_RNN // N_RNN_BLOCKS
CONV_W = 4
LRU_C = 8.0
HEAD_DIM = 64
N_HEADS = D_MODEL // (2 * HEAD_DIM)
V_DIM = 2 * HEAD_DIM
ROT_DIM = HEAD_DIM // 4
ROPE_THETA = 500000.0
D_FF = -(-8 * D_MODEL // (3 * 256)) * 256
PAGE_SIZE = 128
NORM_EPS = 1e-6
SUBLN_EPS = 1e-5
NEG_INF = -1e30
QK_SCALE = HEAD_DIM ** -0.5

V7X_VMEM_BYTES = 64 * 1024 * 1024
V7X_SUBLANES = 8
V7X_LANES = 128
V7X_MXU_DIM = 256

ROW_TILE = 512
FF_CHUNK = V7X_MXU_DIM
N_FF_CHUNKS = D_FF // FF_CHUNK
ATTN_TILE = 512
PAGES_PER_STEP = 8
VMEM_LIMIT = V7X_VMEM_BYTES - 8 * 1024 * 1024


def _cparams(semantics):
    return pltpu.CompilerParams(dimension_semantics=semantics, vmem_limit_bytes=VMEM_LIMIT)


def _const_spec(shape):
    nd = len(shape)
    return pl.BlockSpec(shape, lambda *_: (0,) * nd, pipeline_mode=pl.Buffered(1))


def _rms(x, g, eps):
    ms = jnp.mean(x * x, axis=-1, keepdims=True)
    return x * lax.rsqrt(ms + eps) * g


def _log_sigmoid(x):
    return jnp.minimum(x, 0.0) - jnp.log1p(jnp.exp(-jnp.abs(x)))


def _lru_coeffs(xc, wx_ref, bx, wa_ref, ba, lam):
    xcb = xc.astype(BF16)

    def gate(w_ref, b):
        z = [jnp.dot(xcb[:, n * RNN_BLOCK:(n + 1) * RNN_BLOCK], w_ref[n], preferred_element_type=F32)
             for n in range(N_RNN_BLOCKS)]
        return jax.nn.sigmoid(jnp.concatenate(z, axis=-1) + b)

    gx = gate(wx_ref, bx)
    ga = gate(wa_ref, ba)
    log_a = LRU_C * ga * _log_sigmoid(lam)
    a = jnp.exp(log_a)
    mult = jnp.sqrt(jnp.maximum(-jnp.tanh(log_a) * (a * a + 1.0), 0.0))
    return a, mult * gx * xc


def _rec_prompt_kernel(x_ref, g_ref, win_ref, cw_ref, cb_ref, wx_ref, bx_ref, wa_ref, ba_ref, lam_ref,
                       y_ref, conv_ref, h_ref,
                       xe_ref, a_ref, u_ref, hs_ref, hc_ref, *, tm):
    ti = pl.program_id(1)
    pad = V7X_SUBLANES

    @pl.when(ti == 0)
    def _():
        xe_ref[0:pad, :] = jnp.zeros((pad, D_RNN), F32)
        hc_ref[...] = jnp.zeros_like(hc_ref)

    @pl.when(ti > 0)
    def _():
        xe_ref[0:pad, :] = xe_ref[tm:tm + pad, :]

    n = _rms(x_ref[...], g_ref[...], NORM_EPS).astype(BF16)
    y = jnp.dot(n, win_ref[...], preferred_element_type=F32)
    xe_ref[pad:pad + tm, :] = y[:, :D_RNN]
    gate = y[:, D_RNN:]

    cw = cw_ref[...]
    xc = cb_ref[...]
    for k in range(CONV_W):
        off = pad - (CONV_W - 1) + k
        xc = xc + xe_ref[off:off + tm, :] * cw[k:k + 1, :]

    a, u = _lru_coeffs(xc, wx_ref, bx_ref[...], wa_ref, ba_ref[...], lam_ref[...])
    a_ref[...] = a
    u_ref[...] = u

    row = lax.broadcasted_iota(jnp.int32, (pad, D_RNN), 0)

    def group(gidx, h):
        r0 = pl.multiple_of(gidx * pad, pad)
        ag = a_ref[pl.ds(r0, pad), :]
        ug = u_ref[pl.ds(r0, pad), :]
        for k in (1, 2, 4):
            m = row >= k
            a_s = jnp.where(m, pltpu.roll(ag, k, 0), 1.0)
            u_s = jnp.where(m, pltpu.roll(ug, k, 0), 0.0)
            ug = ug + ag * u_s
            ag = ag * a_s
        hrows = ag * h + ug
        hs_ref[pl.ds(r0, pad), :] = hrows
        return hrows[pad - 1:pad, :]

    h_last = lax.fori_loop(0, tm // pad, group, hc_ref[...])
    hc_ref[...] = h_last

    y_ref[...] = (hs_ref[...] * jax.nn.gelu(gate)).astype(y_ref.dtype)

    @pl.when(ti == pl.num_programs(1) - 1)
    def _():
        conv_ref[0] = xe_ref[pad + tm - (CONV_W - 1):pad + tm, :]
        h_ref[0] = h_last


def _rec_prompt(x, g, win, cw, cb, wx, bx, wa, ba, lam, *, batch, seq):
    tm = ROW_TILE
    nt = seq // tm
    row_spec = pl.BlockSpec((tm, D_MODEL), lambda b, t: (b * nt + t, 0))
    return pl.pallas_call(
        functools.partial(_rec_prompt_kernel, tm=tm),
        grid=(batch, nt),
        in_specs=[row_spec, _const_spec((1, D_MODEL)), _const_spec((D_MODEL, 2 * D_RNN)),
                  _const_spec((CONV_W, D_RNN)), _const_spec((1, D_RNN)),
                  _const_spec((N_RNN_BLOCKS, RNN_BLOCK, RNN_BLOCK)), _const_spec((1, D_RNN)),
                  _const_spec((N_RNN_BLOCKS, RNN_BLOCK, RNN_BLOCK)), _const_spec((1, D_RNN)),
                  _const_spec((1, D_RNN))],
        out_specs=[row_spec,
                   pl.BlockSpec((1, CONV_W - 1, D_RNN), lambda b, t: (b, 0, 0)),
                   pl.BlockSpec((1, 1, D_RNN), lambda b, t: (b, 0, 0))],
        out_shape=[jax.ShapeDtypeStruct((batch * seq, D_RNN), BF16),
                   jax.ShapeDtypeStruct((batch, CONV_W - 1, D_RNN), F32),
                   jax.ShapeDtypeStruct((batch, 1, D_RNN), F32)],
        scratch_shapes=[pltpu.VMEM((tm + 2 * V7X_SUBLANES, D_RNN), F32),
                        pltpu.VMEM((tm, D_RNN), F32), pltpu.VMEM((tm, D_RNN), F32),
                        pltpu.VMEM((tm, D_RNN), F32), pltpu.VMEM((1, D_RNN), F32)],
        compiler_params=_cparams(("arbitrary", "arbitrary")),
        name="rec_prompt",
    )(x, g, win, cw, cb, wx, bx, wa, ba, lam)


def _rec_sample_kernel(x_ref, sc_ref, h0_ref, g_ref, win_ref, cw_ref, cb_ref, wx_ref, bx_ref, wa_ref, ba_ref,
                       lam_ref, y_ref, conv_ref, h_ref, *, nb, nt):
    n = _rms(x_ref[...], g_ref[...], NORM_EPS).astype(BF16)
    y = jnp.dot(n, win_ref[...], preferred_element_type=F32)
    gate = y[:, D_RNN:]

    nbuf = CONV_W - 1
    xe = [sc_ref[k * nb:(k + 1) * nb, :] for k in range(nbuf)]
    xe += [y[t * nb:(t + 1) * nb, :D_RNN] for t in range(nt)]
    cw = cw_ref[...]
    slabs = []
    for t in range(nt):
        xc = cb_ref[...]
        for k in range(CONV_W):
            xc = xc + xe[t + k] * cw[k:k + 1, :]
        slabs.append(xc)
    xc = jnp.concatenate(slabs, axis=0)

    a, u = _lru_coeffs(xc, wx_ref, bx_ref[...], wa_ref, ba_ref[...], lam_ref[...])
    h = h0_ref[...]
    for t in range(nt):
        rows = slice(t * nb, (t + 1) * nb)
        h = a[rows] * h + u[rows]
        y_ref[rows, :] = h * jax.nn.gelu(gate[rows])
    h_ref[...] = h
    for k in range(nbuf):
        conv_ref[k * nb:(k + 1) * nb, :] = xe[len(xe) - nbuf + k]


def _rec_sample(x, sc, h0, g, win, cw, cb, wx, bx, wa, ba, lam, *, nb, nt):
    m = nb * nt
    nbuf = CONV_W - 1
    return pl.pallas_call(
        functools.partial(_rec_sample_kernel, nb=nb, nt=nt),
        out_shape=[jax.ShapeDtypeStruct((m, D_RNN), F32),
                   jax.ShapeDtypeStruct((nb * nbuf, D_RNN), F32),
                   jax.ShapeDtypeStruct((nb, D_RNN), F32)],
        compiler_params=pltpu.CompilerParams(vmem_limit_bytes=VMEM_LIMIT),
        name="rec_sample",
    )(x, sc, h0, g, win, cw, cb, wx, bx, wa, ba, lam)


def _rope(xh, c, sneg, spos):
    half = ROT_DIM // 2
    return xh * c + pltpu.roll(xh, V7X_LANES - half, 1) * sneg + pltpu.roll(xh, half, 1) * spos


def _rope_tables(pos):
    half = ROT_DIM // 2
    inv = jnp.power(ROPE_THETA, -jnp.arange(half, dtype=F32) * 2.0 / ROT_DIM)
    ang = pos[:, None] * inv[None, :]
    cos, sin = jnp.cos(ang), jnp.sin(ang)
    n = pos.shape[0]
    ones = jnp.ones((n, HEAD_DIM - ROT_DIM), F32)
    zeros = jnp.zeros((n, HEAD_DIM - ROT_DIM), F32)
    zh = jnp.zeros((n, half), F32)
    c = jnp.concatenate([cos, cos, ones], axis=-1)
    sneg = jnp.concatenate([-sin, zh, zeros], axis=-1)
    spos = jnp.concatenate([zh, sin, zeros], axis=-1)
    rep = V7X_LANES // HEAD_DIM
    return tuple(jnp.tile(t, (1, rep)) for t in (c, sneg, spos))


def _qkv_sample_kernel(x_ref, g_ref, w_ref, cos_ref, sneg_ref, spos_ref, q_ref, k_ref, v_ref):
    n = _rms(x_ref[...], g_ref[...], NORM_EPS).astype(BF16)
    qkv = jnp.dot(n, w_ref[...], preferred_element_type=F32)
    tabs = (cos_ref[...], sneg_ref[...], spos_ref[...])
    for h in range(N_HEADS):
        lo, hi = h * V_DIM, (h + 1) * V_DIM
        q_ref[:, lo:hi] = _rope(qkv[:, lo:hi], *tabs) * QK_SCALE
        k_ref[:, lo:hi] = _rope(qkv[:, D_MODEL + lo:D_MODEL + hi], *tabs)
    v_ref[...] = qkv[:, 2 * D_MODEL:]


def _qkv_sample(x, g, w, cos, sneg, spos):
    out = jax.ShapeDtypeStruct((x.shape[0], D_MODEL), F32)
    return pl.pallas_call(
        _qkv_sample_kernel,
        out_shape=[out, out, out],
        compiler_params=pltpu.CompilerParams(vmem_limit_bytes=VMEM_LIMIT),
        name="qkv_sample",
    )(x, g, w, cos, sneg, spos)


def _qkv_prompt_kernel(x_ref, g_ref, w_ref, cos_ref, sneg_ref, spos_ref, *rest):
    q_ref, kt_ref, vo_ref, ktb_ref, vb_ref = rest[-5:]
    n = _rms(x_ref[...], g_ref[...], NORM_EPS).astype(BF16)
    qkv = jnp.dot(n, w_ref[...], preferred_element_type=F32)
    tabs = (cos_ref[...], sneg_ref[...], spos_ref[...])
    for h in range(N_HEADS):
        lo, hi = h * V_DIM, (h + 1) * V_DIM
        q_ref[:, lo:hi] = (_rope(qkv[:, lo:hi], *tabs) * QK_SCALE).astype(q_ref.dtype)
        kt = _rope(qkv[:, D_MODEL + lo:D_MODEL + hi], *tabs).T
        kt_ref[lo:hi, :] = kt
        ktb_ref[lo:hi, :] = kt.astype(BF16)
        vh = qkv[:, 2 * D_MODEL + lo:2 * D_MODEL + hi]
        vo_ref[:, h, :] = vh
        vb_ref[h] = vh.astype(BF16)


def _qkv_prompt(x, g, w, cos, sneg, spos, kt_all, v_all, *, layer, batch, seq):
    tm = ROW_TILE
    nt = seq // tm
    row_spec = pl.BlockSpec((tm, D_MODEL), lambda b, t: (b * nt + t, 0))
    tab_spec = pl.BlockSpec((tm, V7X_LANES), lambda b, t: (t, 0))
    in_specs = [row_spec, _const_spec((1, D_MODEL)), _const_spec((D_MODEL, 3 * D_MODEL)),
                tab_spec, tab_spec, tab_spec]
    args = [x, g, w, cos, sneg, spos]
    aliases = {}
    if kt_all is not None:
        in_specs += [pl.BlockSpec(memory_space=pl.ANY)] * 2
        aliases = {len(args): 1, len(args) + 1: 2}
        args += [kt_all, v_all]
    return pl.pallas_call(
        _qkv_prompt_kernel,
        grid=(batch, nt),
        in_specs=in_specs,
        out_specs=[row_spec,
                   pl.BlockSpec((None, None, D_MODEL, tm), lambda b, t: (layer, b, 0, t)),
                   pl.BlockSpec((None, None, tm, N_HEADS, V_DIM), lambda b, t: (layer, b, t, 0, 0)),
                   pl.BlockSpec((None, D_MODEL, tm), lambda b, t: (b, 0, t)),
                   pl.BlockSpec((None, N_HEADS, tm, V_DIM), lambda b, t: (b, 0, t, 0))],
        out_shape=[jax.ShapeDtypeStruct((batch * seq, D_MODEL), BF16),
                   jax.ShapeDtypeStruct((N_ATTN_LAYERS, batch, D_MODEL, seq), F32),
                   jax.ShapeDtypeStruct((N_ATTN_LAYERS, batch, seq, N_HEADS, V_DIM), F32),
                   jax.ShapeDtypeStruct((batch, D_MODEL, seq), BF16),
                   jax.ShapeDtypeStruct((batch, N_HEADS, seq, V_DIM), BF16)],
        input_output_aliases=aliases,
        compiler_params=_cparams(("arbitrary", "arbitrary")),
        name="qkv_prompt",
    )(*args)


def _diff_lambda(lq1, lk1, lq2, lk2, lam_init):
    s1 = jnp.sum(lq1 * lk1, axis=-1, keepdims=True)
    s2 = jnp.sum(lq2 * lk2, axis=-1, keepdims=True)
    return jnp.exp(s1) - jnp.exp(s2) + lam_init


def _attn_kernel(lq1_ref, lk1_ref, lq2_ref, lk2_ref, sg_ref, q_ref, kt_ref, v_ref, o_ref,
                 q2_ref, m_ref, l_ref, acc_ref, *, tile, lam_init):
    qi = pl.program_id(2)
    ki = pl.program_id(3)

    @pl.when(ki == 0)
    def _():
        q = q_ref[...]
        lane = lax.broadcasted_iota(jnp.int32, q.shape, 1)
        zero = jnp.zeros_like(q)
        q2_ref[0:tile, :] = jnp.where(lane < HEAD_DIM, q, zero)
        q2_ref[tile:2 * tile, :] = jnp.where(lane >= HEAD_DIM, q, zero)
        m_ref[...] = jnp.full_like(m_ref, NEG_INF)
        l_ref[...] = jnp.zeros_like(l_ref)
        acc_ref[...] = jnp.zeros_like(acc_ref)

    @pl.when(ki <= qi)
    def _():
        s = jnp.dot(q2_ref[...], kt_ref[...], preferred_element_type=F32)
        row = lax.broadcasted_iota(jnp.int32, s.shape, 0)
        col = lax.broadcasted_iota(jnp.int32, s.shape, 1)
        q_pos = jnp.where(row >= tile, row - tile, row) + qi * tile
        s = jnp.where(col + ki * tile <= q_pos, s, NEG_INF)
        m_prev = m_ref[...]
        m_new = jnp.maximum(m_prev, jnp.max(s, axis=-1, keepdims=True))
        alpha = jnp.exp(m_prev - m_new)
        p = jnp.exp(s - m_new)
        l_ref[...] = alpha * l_ref[...] + jnp.sum(p, axis=-1, keepdims=True)
        acc_ref[...] = alpha * acc_ref[...] + jnp.dot(p.astype(BF16), v_ref[...], preferred_element_type=F32)
        m_ref[...] = m_new

    @pl.when(ki == qi)
    def _():
        lam = _diff_lambda(lq1_ref[...], lk1_ref[...], lq2_ref[...], lk2_ref[...], lam_init)
        o = acc_ref[0:tile, :] / l_ref[0:tile, :] - lam * (acc_ref[tile:2 * tile, :] / l_ref[tile:2 * tile, :])
        o_ref[...] = (_rms(o, sg_ref[...], SUBLN_EPS) * (1.0 - lam_init)).astype(o_ref.dtype)


def _attn_prompt(lq1, lk1, lq2, lk2, sg, q, ktb, vb, *, batch, seq, lam_init):
    tile = ATTN_TILE
    nq = seq // tile
    lam_spec = _const_spec((1, HEAD_DIM))
    q_spec = pl.BlockSpec((tile, V_DIM), lambda b, h, i, j: (b * nq + i, h))
    kt_spec = pl.BlockSpec((None, V_DIM, tile), lambda b, h, i, j: (b, h, jnp.minimum(j, i)))
    v_spec = pl.BlockSpec((None, None, tile, V_DIM), lambda b, h, i, j: (b, h, jnp.minimum(j, i), 0))
    return pl.pallas_call(
        functools.partial(_attn_kernel, tile=tile, lam_init=lam_init),
        grid=(batch, N_HEADS, nq, nq),
        in_specs=[lam_spec, lam_spec, lam_spec, lam_spec, _const_spec((1, V_DIM)), q_spec, kt_spec, v_spec],
        out_specs=q_spec,
        out_shape=jax.ShapeDtypeStruct((batch * seq, D_MODEL), BF16),
        scratch_shapes=[pltpu.VMEM((2 * tile, V_DIM), BF16), pltpu.VMEM((2 * tile, 1), F32),
                        pltpu.VMEM((2 * tile, 1), F32), pltpu.VMEM((2 * tile, V_DIM), F32)],
        compiler_params=_cparams(("arbitrary",) * 4),
        name="attn_prompt",
    )(lq1, lk1, lq2, lk2, sg, q, ktb, vb)


def _attn_sample_kernel(pt_ref, lq1_ref, lk1_ref, lq2_ref, lk2_ref, sg_ref, q_ref, kn_ref, vn_ref, *rest,
                        nt, lam_init):
    del pt_ref
    npp = PAGES_PER_STEP
    k_refs, v_refs = rest[:npp], rest[npp:2 * npp]
    o_ref = rest[2 * npp]
    qt_ref, qtb_ref, s_ref, p_ref, pn_ref, l_ref, acc_ref = rest[2 * npp + 1:]
    ph = pl.program_id(1)
    j = pl.program_id(2)
    nrow = nt * N_HEADS

    @pl.when((ph == 0) & (j == 0))
    def _():
        q = q_ref[0]
        sub = lax.broadcasted_iota(jnp.int32, (N_HEADS, D_MODEL), 0)
        col = lax.broadcasted_iota(jnp.int32, (N_HEADS, D_MODEL), 1)
        for c in range(2):
            keep = (col // V_DIM == sub) & ((col // HEAD_DIM) % 2 == c)
            for t in range(nt):
                r0 = c * nrow + t * N_HEADS
                qt_ref[r0:r0 + N_HEADS, :] = jnp.where(keep, jnp.broadcast_to(q[t:t + 1, :], (N_HEADS, D_MODEL)), 0.0)
        qtb_ref[...] = qt_ref[...].astype(BF16)

    @pl.when(ph == 0)
    def _():
        qtb = qtb_ref[...]
        for i in range(npp):
            kt = k_refs[i][...].astype(BF16)
            s_ref[j, :, i * PAGE_SIZE:(i + 1) * PAGE_SIZE] = jnp.dot(qtb, kt, preferred_element_type=F32)

    @pl.when((ph == 1) & (j == 0))
    def _():
        qt = qt_ref[...]
        kn = kn_ref[0]
        r = lax.broadcasted_iota(jnp.int32, (2 * nrow, 1), 0)
        t_row = (r // N_HEADS) % nt
        sn = []
        for t in range(nt):
            st = jnp.sum(qt * kn[t:t + 1, :], axis=-1, keepdims=True)
            sn.append(jnp.where(t <= t_row, st, NEG_INF))
        s_all = s_ref[...]
        m = jnp.max(jnp.max(s_all, axis=0), axis=-1, keepdims=True)
        for st in sn:
            m = jnp.maximum(m, st)
        p = jnp.exp(s_all - m[None])
        l = jnp.sum(jnp.sum(p, axis=0), axis=-1, keepdims=True)
        p_ref[...] = p.astype(BF16)
        for t in range(nt):
            pt = jnp.exp(sn[t] - m)
            pn_ref[t] = pt
            l = l + pt
        l_ref[...] = l
        acc_ref[...] = jnp.zeros_like(acc_ref)

    @pl.when(ph == 1)
    def _():
        acc = acc_ref[...]
        for i in range(npp):
            vb = jnp.concatenate([v_refs[i][:, h, :] for h in range(N_HEADS)], axis=-1).astype(BF16)
            acc = acc + jnp.dot(p_ref[j, :, i * PAGE_SIZE:(i + 1) * PAGE_SIZE], vb, preferred_element_type=F32)
        acc_ref[...] = acc

    @pl.when((ph == 1) & (j == pl.num_programs(2) - 1))
    def _():
        acc = acc_ref[...]
        vn = vn_ref[0]
        for t in range(nt):
            acc = acc + pn_ref[t] * vn[t:t + 1, :]
        acc = acc / l_ref[...]
        lam = _diff_lambda(lq1_ref[...], lk1_ref[...], lq2_ref[...], lk2_ref[...], lam_init)
        o = acc[0:nrow, :] - lam * acc[nrow:2 * nrow, :]
        sub = lax.broadcasted_iota(jnp.int32, (nrow, D_MODEL), 0) % N_HEADS
        col = lax.broadcasted_iota(jnp.int32, (nrow, D_MODEL), 1)
        o = jnp.where(col // V_DIM == sub, o, 0.0)
        ms = jnp.sum(o * o, axis=-1, keepdims=True) * (1.0 / V_DIM)
        o = o * lax.rsqrt(ms + SUBLN_EPS)
        for t in range(nt):
            ot = jnp.sum(o[t * N_HEADS:(t + 1) * N_HEADS, :], axis=0, keepdims=True)
            o_ref[0, t:t + 1, :] = ot * sg_ref[...] * (1.0 - lam_init)


def _attn_sample(page_table, lq1, lk1, lq2, lk2, sg_tiled, q, kn, vn, cache_k, cache_v, *, layer, lam_init):
    nb, nt, _ = q.shape
    n_pages = page_table.shape[1]
    npp = PAGES_PER_STEP
    nj = n_pages // npp
    nrow2 = 2 * nt * N_HEADS

    def k_map(i):
        return lambda b, ph, j, pt: (layer, pt[b, jnp.where(ph == 0, j, nj - 1) * npp + i], 0, 0)

    def v_map(i):
        return lambda b, ph, j, pt: (layer, pt[b, jnp.where(ph == 1, j, 0) * npp + i], 0, 0, 0)

    k_block = (None, None, D_MODEL, PAGE_SIZE)
    v_block = (None, None, PAGE_SIZE, N_HEADS, V_DIM)
    seq_spec = pl.BlockSpec((1, nt, D_MODEL), lambda b, ph, j, pt: (b, 0, 0))
    lam_spec = pl.BlockSpec((1, HEAD_DIM), lambda b, ph, j, pt: (0, 0))
    grid_spec = pltpu.PrefetchScalarGridSpec(
        num_scalar_prefetch=1,
        grid=(nb, 2, nj),
        in_specs=[lam_spec, lam_spec, lam_spec, lam_spec,
                  pl.BlockSpec((1, D_MODEL), lambda b, ph, j, pt: (0, 0)),
                  seq_spec, seq_spec, seq_spec]
                 + [pl.BlockSpec(k_block, k_map(i)) for i in range(npp)]
                 + [pl.BlockSpec(v_block, v_map(i)) for i in range(npp)],
        out_specs=seq_spec,
        scratch_shapes=[pltpu.VMEM((nrow2, D_MODEL), F32), pltpu.VMEM((nrow2, D_MODEL), BF16),
                        pltpu.VMEM((nj, nrow2, npp * PAGE_SIZE), F32),
                        pltpu.VMEM((nj, nrow2, npp * PAGE_SIZE), BF16),
                        pltpu.VMEM((nt, nrow2, 1), F32), pltpu.VMEM((nrow2, 1), F32),
                        pltpu.VMEM((nrow2, D_MODEL), F32)],
    )
    ck = jnp.transpose(cache_k, (0, 1, 3, 4, 5, 2)).reshape(cache_k.shape[0], cache_k.shape[1], D_MODEL, PAGE_SIZE)
    return pl.pallas_call(
        functools.partial(_attn_sample_kernel, nt=nt, lam_init=lam_init),
        grid_spec=grid_spec,
        out_shape=jax.ShapeDtypeStruct((nb, nt, D_MODEL), F32),
        compiler_params=_cparams(("arbitrary",) * 3),
        name="attn_sample",
    )(page_table, lq1, lk1, lq2, lk2, sg_tiled, q, kn, vn, *([ck] * npp), *([cache_v] * npp))


def _post_kernel(x_ref, y_ref, wp_ref, g_ref, wi_ref, wo_ref, gf_ref, o_ref, *, final):
    x = x_ref[...] + jnp.dot(y_ref[...].astype(BF16), wp_ref[...], preferred_element_type=F32)
    n = _rms(x, g_ref[...], NORM_EPS).astype(BF16)
    out = x
    for c in range(N_FF_CHUNKS):
        lo, hi = c * FF_CHUNK, (c + 1) * FF_CHUNK
        gc = jnp.dot(n, wi_ref[:, lo:hi], preferred_element_type=F32)
        uc = jnp.dot(n, wi_ref[:, D_FF + lo:D_FF + hi], preferred_element_type=F32)
        act = (gc * jax.nn.sigmoid(gc) * uc).astype(BF16)
        out = out + jnp.dot(act, wo_ref[lo:hi, :], preferred_element_type=F32)
    if final:
        out = _rms(out, gf_ref[...], NORM_EPS)
    o_ref[...] = out


def _post(x, y, wp, g, wi, wo, gf, *, tm, final):
    m = x.shape[0]
    row_spec = pl.BlockSpec((tm, D_MODEL), lambda i: (i, 0))
    return pl.pallas_call(
        functools.partial(_post_kernel, final=final),
        grid=(m // tm,),
        in_specs=[row_spec, row_spec, _const_spec((D_MODEL, D_MODEL)), _const_spec((1, D_MODEL)),
                  _const_spec((D_MODEL, 2 * D_FF)), _const_spec((D_FF, D_MODEL)), _const_spec((1, D_MODEL))],
        out_specs=row_spec,
        out_shape=jax.ShapeDtypeStruct((m, D_MODEL), F32),
        compiler_params=_cparams(("arbitrary",)),
        name="post",
    )(x, y, wp, g, wi, wo, gf)


def kernel(x_prompt, x_sample, cache_k, cache_v, page_table, state_conv, state_h, norm_mix, w_rec_in, conv_w,
           conv_b, gate_x_w, gate_x_b, gate_a_w, gate_a_b, lru_lambda, w_rec_out, w_qkv, lambda_q1, lambda_k1,
           lambda_q2, lambda_k2, subln_g, w_attn_out, norm_ffn, w_ffn_in, w_ffn_out, norm_final):
    B, T, _ = x_prompt.shape
    DB, Tn, _ = x_sample.shape
    past = page_table.shape[1] * PAGE_SIZE
    row = lambda a: a.reshape(1, -1)

    hp = x_prompt.reshape(B * T, D_MODEL)
    hs = x_sample.reshape(DB * Tn, D_MODEL)
    tabs_p = _rope_tables(jnp.arange(T, dtype=F32))
    tabs_s = _rope_tables(jnp.tile(past + jnp.arange(Tn, dtype=F32), DB))

    kt_all = v_all = None
    pconv, ph = [], []
    sk, sv, sconv, sh = [], [], [], []
    for i in range(DEPTH):
        g_mix = row(norm_mix[i])
        if i % N_MIXERS == 0:
            r = i // N_MIXERS
            w = (w_rec_in[r].astype(BF16), conv_w[r], row(conv_b[r]),
                 gate_x_w[r].astype(BF16), row(gate_x_b[r]), gate_a_w[r].astype(BF16), row(gate_a_b[r]),
                 row(lru_lambda[r]))
            yp, c_p, h_p = _rec_prompt(hp, g_mix, *w, batch=B, seq=T)
            to_tm = lambda t, n: t.reshape(DB, n, D_RNN).swapaxes(0, 1).reshape(n * DB, D_RNN)
            from_tm = lambda t, n: t.reshape(n, DB, D_RNN).swapaxes(0, 1)
            ys, c_s, h_s = _rec_sample(to_tm(hs, Tn), to_tm(state_conv[r], CONV_W - 1), state_h[r], g_mix, *w,
                                       nb=DB, nt=Tn)
            ys = from_tm(ys, Tn).reshape(DB * Tn, D_RNN)
            pconv.append(c_p); ph.append(h_p.reshape(B, D_RNN))
            sconv.append(from_tm(c_s, CONV_W - 1)); sh.append(h_s)
            w_proj = w_rec_out[r].astype(BF16)
        else:
            a = i // N_MIXERS
            lam_init = 0.8 - 0.6 * math.exp(-0.3 * i)
            lams = (row(lambda_q1[a]), row(lambda_k1[a]), row(lambda_q2[a]), row(lambda_k2[a]))
            wq = w_qkv[a].astype(BF16)
            q_p, kt_all, v_all, ktb, vb = _qkv_prompt(hp, g_mix, wq, *tabs_p, kt_all, v_all,
                                                      layer=a, batch=B, seq=T)
            yp = _attn_prompt(*lams, row(subln_g[a]), q_p, ktb, vb, batch=B, seq=T, lam_init=lam_init)
            q_s, k_s, v_s = _qkv_sample(hs, g_mix, wq, *tabs_s)
            seq3 = lambda t: t.reshape(DB, Tn, D_MODEL)
            ys = _attn_sample(page_table, *lams, row(jnp.tile(subln_g[a], N_HEADS)), seq3(q_s), seq3(k_s),
                              seq3(v_s), cache_k, cache_v, layer=a, lam_init=lam_init).reshape(DB * Tn, D_MODEL)
            sk.append(k_s.reshape(DB, Tn, N_HEADS, 2, HEAD_DIM)); sv.append(v_s.reshape(DB, Tn, N_HEADS, V_DIM))
            w_proj = w_attn_out[a].astype(BF16)

        final = i == DEPTH - 1
        post_w = (w_proj, row(norm_ffn[i]), w_ffn_in[i].astype(BF16), w_ffn_out[i].astype(BF16), row(norm_final))
        hp = _post(hp, yp, *post_w, tm=ROW_TILE, final=final)
        hs = _post(hs, ys, *post_w, tm=DB * Tn, final=final)

    prompt_k = jnp.transpose(kt_all.reshape(N_ATTN_LAYERS, B, N_HEADS, 2, HEAD_DIM, T), (0, 1, 5, 2, 3, 4))
    return (hp.reshape(B, T, D_MODEL), hs.reshape(DB, Tn, D_MODEL),
            prompt_k, v_all, jnp.stack(pconv), jnp.stack(ph),
            jnp.stack(sk), jnp.stack(sv), jnp.stack(sconv), jnp.stack(sh))
```

```python
import functools
import math

import jax
import jax.numpy as jnp
from jax import lax
from jax.experimental import pallas as pl
from jax.experimental.pallas import tpu as pltpu

F32 = jnp.float32
BF16 = jnp.bfloat16

D_MODEL = 1024
DEPTH = 4
N_MIXERS = 2
N_ATTN_LAYERS = DEPTH // N_MIXERS
D_RNN = D_MODEL
N_RNN_BLOCKS = 4
RNN_BLOCK = D_RNN // N_RNN_BLOCKS
CONV_W = 4
LRU_C = 8.0
HEAD_DIM = 64
N_HEADS = D_MODEL // (2 * HEAD_DIM)
V_DIM = 2 * HEAD_DIM
ROT_DIM = HEAD_DIM // 4
ROPE_THETA = 500000.0
D_FF = -(-8 * D_MODEL // (3 * 256)) * 256
PAGE_SIZE = 128
NORM_EPS = 1e-6
SUBLN_EPS = 1e-5
NEG_INF = -1e30
QK_SCALE = HEAD_DIM ** -0.5 * math.log2(math.e)

V7X_VMEM_BYTES = 64 * 1024 * 1024
V7X_SUBLANES = 8
V7X_LANES = 128
V7X_MXU_DIM = 256

ROW_TILE = 512
FF_CHUNK = V7X_MXU_DIM
N_FF_CHUNKS = D_FF // FF_CHUNK
ATTN_TILE = 512
PAGES_PER_STEP = 8
VMEM_LIMIT = V7X_VMEM_BYTES - 8 * 1024 * 1024


def _cparams(semantics):
    return pltpu.CompilerParams(dimension_semantics=semantics, vmem_limit_bytes=VMEM_LIMIT)


def _const_spec(shape):
    nd = len(shape)
    return pl.BlockSpec(shape, lambda *_: (0,) * nd, pipeline_mode=pl.Buffered(1))


def _rms(x, g, eps):
    ms = jnp.mean(x * x, axis=-1, keepdims=True)
    return x * lax.rsqrt(ms + eps) * g


def _log_sigmoid(x):
    return jnp.minimum(x, 0.0) - jnp.log1p(jnp.exp(-jnp.abs(x)))


def _lru_coeffs(xc, wx_ref, bx, wa_ref, ba, lam):
    xcb = xc.astype(BF16)

    def gate(w_ref, b):
        z = [jnp.dot(xcb[:, n * RNN_BLOCK:(n + 1) * RNN_BLOCK], w_ref[n], preferred_element_type=F32)
             for n in range(N_RNN_BLOCKS)]
        return jax.nn.sigmoid(jnp.concatenate(z, axis=-1) + b)

    gx = gate(wx_ref, bx)
    ga = gate(wa_ref, ba)
    log_a = LRU_C * ga * _log_sigmoid(lam)
    a = jnp.exp(log_a)
    mult = jnp.sqrt(jnp.maximum(-jnp.tanh(log_a) * (a * a + 1.0), 0.0))
    return a, mult * gx * xc


def _rec_prompt_kernel(x_ref, g_ref, win_ref, cw_ref, cb_ref, wx_ref, bx_ref, wa_ref, ba_ref, lam_ref,
                       y_ref, conv_ref, h_ref,
                       xe_ref, a_ref, u_ref, hs_ref, hc_ref, *, tm):
    ti = pl.program_id(1)
    pad = V7X_SUBLANES

    @pl.when(ti == 0)
    def _():
        xe_ref[0:pad, :] = jnp.zeros((pad, D_RNN), F32)
        hc_ref[...] = jnp.zeros_like(hc_ref)

    @pl.when(ti > 0)
    def _():
        xe_ref[0:pad, :] = xe_ref[tm:tm + pad, :]

    n = _rms(x_ref[...], g_ref[...], NORM_EPS).astype(BF16)
    y = jnp.dot(n, win_ref[...], preferred_element_type=F32)
    xe_ref[pad:pad + tm, :] = y[:, :D_RNN]
    gate = y[:, D_RNN:]

    cw = cw_ref[...]
    xc = cb_ref[...]
    for k in range(CONV_W):
        off = pad - (CONV_W - 1) + k
        xc = xc + xe_ref[off:off + tm, :] * cw[k:k + 1, :]

    a, u = _lru_coeffs(xc, wx_ref, bx_ref[...], wa_ref, ba_ref[...], lam_ref[...])
    a_ref[...] = a
    u_ref[...] = u

    row = lax.broadcasted_iota(jnp.int32, (pad, D_RNN), 0)

    def group(gidx, h):
        r0 = pl.multiple_of(gidx * pad, pad)
        ag = a_ref[pl.ds(r0, pad), :]
        ug = u_ref[pl.ds(r0, pad), :]
        for k in (1, 2, 4):
            m = row >= k
            a_s = jnp.where(m, pltpu.roll(ag, k, 0), 1.0)
            u_s = jnp.where(m, pltpu.roll(ug, k, 0), 0.0)
            ug = ug + ag * u_s
            ag = ag * a_s
        hrows = ag * h + ug
        hs_ref[pl.ds(r0, pad), :] = hrows
        return hrows[pad - 1:pad, :]

    h_last = lax.fori_loop(0, tm // pad, group, hc_ref[...])
    hc_ref[...] = h_last

    y_ref[...] = (hs_ref[...] * jax.nn.gelu(gate)).astype(y_ref.dtype)

    @pl.when(ti == pl.num_programs(1) - 1)
    def _():
        conv_ref[0] = xe_ref[pad + tm - (CONV_W - 1):pad + tm, :]
        h_ref[0] = h_last


def _rec_prompt(x, g, win, cw, cb, wx, bx, wa, ba, lam, *, batch, seq):
    tm = ROW_TILE
    nt = seq // tm
    row_spec = pl.BlockSpec((tm, D_MODEL), lambda b, t: (b * nt + t, 0))
    return pl.pallas_call(
        functools.partial(_rec_prompt_kernel, tm=tm),
        grid=(batch, nt),
        in_specs=[row_spec, _const_spec((1, D_MODEL)), _const_spec((D_MODEL, 2 * D_RNN)),
                  _const_spec((CONV_W, D_RNN)), _const_spec((1, D_RNN)),
                  _const_spec((N_RNN_BLOCKS, RNN_BLOCK, RNN_BLOCK)), _const_spec((1, D_RNN)),
                  _const_spec((N_RNN_BLOCKS, RNN_BLOCK, RNN_BLOCK)), _const_spec((1, D_RNN)),
                  _const_spec((1, D_RNN))],
        out_specs=[row_spec,
                   pl.BlockSpec((1, CONV_W - 1, D_RNN), lambda b, t: (b, 0, 0)),
                   pl.BlockSpec((1, 1, D_RNN), lambda b, t: (b, 0, 0))],
        out_shape=[jax.ShapeDtypeStruct((batch * seq, D_RNN), BF16),
                   jax.ShapeDtypeStruct((batch, CONV_W - 1, D_RNN), F32),
                   jax.ShapeDtypeStruct((batch, 1, D_RNN), F32)],
        scratch_shapes=[pltpu.VMEM((tm + 2 * V7X_SUBLANES, D_RNN), F32),
                        pltpu.VMEM((tm, D_RNN), F32), pltpu.VMEM((tm, D_RNN), F32),
                        pltpu.VMEM((tm, D_RNN), F32), pltpu.VMEM((1, D_RNN), F32)],
        compiler_params=_cparams(("arbitrary", "arbitrary")),
        name="rec_prompt",
    )(x, g, win, cw, cb, wx, bx, wa, ba, lam)


def _rec_sample_kernel(x_ref, sc_ref, h0_ref, g_ref, win_ref, cw_ref, cb_ref, wx_ref, bx_ref, wa_ref, ba_ref,
                       lam_ref, y_ref, conv_ref, h_ref, *, nb, nt):
    n = _rms(x_ref[...], g_ref[...], NORM_EPS).astype(BF16)
    y = jnp.dot(n, win_ref[...], preferred_element_type=F32)
    gate = y[:, D_RNN:]

    nbuf = CONV_W - 1
    xe = [sc_ref[k * nb:(k + 1) * nb, :] for k in range(nbuf)]
    xe += [y[t * nb:(t + 1) * nb, :D_RNN] for t in range(nt)]
    cw = cw_ref[...]
    slabs = []
    for t in range(nt):
        xc = cb_ref[...]
        for k in range(CONV_W):
            xc = xc + xe[t + k] * cw[k:k + 1, :]
        slabs.append(xc)
    xc = jnp.concatenate(slabs, axis=0)

    a, u = _lru_coeffs(xc, wx_ref, bx_ref[...], wa_ref, ba_ref[...], lam_ref[...])
    h = h0_ref[...]
    for t in range(nt):
        rows = slice(t * nb, (t + 1) * nb)
        h = a[rows] * h + u[rows]
        y_ref[rows, :] = h * jax.nn.gelu(gate[rows])
    h_ref[...] = h
    for k in range(nbuf):
        conv_ref[k * nb:(k + 1) * nb, :] = xe[len(xe) - nbuf + k]


def _rec_sample(x, sc, h0, g, win, cw, cb, wx, bx, wa, ba, lam, *, nb, nt):
    m = nb * nt
    nbuf = CONV_W - 1
    return pl.pallas_call(
        functools.partial(_rec_sample_kernel, nb=nb, nt=nt),
        out_shape=[jax.ShapeDtypeStruct((m, D_RNN), F32),
                   jax.ShapeDtypeStruct((nb * nbuf, D_RNN), F32),
                   jax.ShapeDtypeStruct((nb, D_RNN), F32)],
        compiler_params=pltpu.CompilerParams(vmem_limit_bytes=VMEM_LIMIT),
        name="rec_sample",
    )(x, sc, h0, g, win, cw, cb, wx, bx, wa, ba, lam)


def _rope(xh, c, sneg, spos):
    half = ROT_DIM // 2
    return xh * c + pltpu.roll(xh, V7X_LANES - half, 1) * sneg + pltpu.roll(xh, half, 1) * spos


def _rope_tables(pos):
    half = ROT_DIM // 2
    inv = jnp.power(ROPE_THETA, -jnp.arange(half, dtype=F32) * 2.0 / ROT_DIM)
    ang = pos[:, None] * inv[None, :]
    cos, sin = jnp.cos(ang), jnp.sin(ang)
    n = pos.shape[0]
    ones = jnp.ones((n, HEAD_DIM - ROT_DIM), F32)
    zeros = jnp.zeros((n, HEAD_DIM - ROT_DIM), F32)
    zh = jnp.zeros((n, half), F32)
    c = jnp.concatenate([cos, cos, ones], axis=-1)
    sneg = jnp.concatenate([-sin, zh, zeros], axis=-1)
    spos = jnp.concatenate([zh, sin, zeros], axis=-1)
    rep = V7X_LANES // HEAD_DIM
    return tuple(jnp.tile(t, (1, rep)) for t in (c, sneg, spos))


def _qkv_sample_kernel(x_ref, g_ref, w_ref, cos_ref, sneg_ref, spos_ref, q_ref, k_ref, v_ref):
    n = _rms(x_ref[...], g_ref[...], NORM_EPS).astype(BF16)
    qkv = jnp.dot(n, w_ref[...], preferred_element_type=F32)
    tabs = (cos_ref[...], sneg_ref[...], spos_ref[...])
    for h in range(N_HEADS):
        lo, hi = h * V_DIM, (h + 1) * V_DIM
        q_ref[:, lo:hi] = _rope(qkv[:, lo:hi], *tabs) * QK_SCALE
        k_ref[:, lo:hi] = _rope(qkv[:, D_MODEL + lo:D_MODEL + hi], *tabs)
    v_ref[...] = qkv[:, 2 * D_MODEL:]


def _qkv_sample(x, g, w, cos, sneg, spos):
    out = jax.ShapeDtypeStruct((x.shape[0], D_MODEL), F32)
    return pl.pallas_call(
        _qkv_sample_kernel,
        out_shape=[out, out, out],
        compiler_params=pltpu.CompilerParams(vmem_limit_bytes=VMEM_LIMIT),
        name="qkv_sample",
    )(x, g, w, cos, sneg, spos)


def _qkv_prompt_kernel(x_ref, g_ref, w_ref, cos_ref, sneg_ref, spos_ref, *rest):
    q_ref, kt_ref, vo_ref, ktb_ref, vb_ref = rest[-5:]
    n = _rms(x_ref[...], g_ref[...], NORM_EPS).astype(BF16)
    qkv = jnp.dot(n, w_ref[...], preferred_element_type=F32)
    tabs = (cos_ref[...], sneg_ref[...], spos_ref[...])
    for h in range(N_HEADS):
        lo, hi = h * V_DIM, (h + 1) * V_DIM
        q_ref[:, lo:hi] = (_rope(qkv[:, lo:hi], *tabs) * QK_SCALE).astype(q_ref.dtype)
        kt = _rope(qkv[:, D_MODEL + lo:D_MODEL + hi], *tabs).T
        kt_ref[lo:hi, :] = kt
        ktb_ref[lo:hi, :] = kt.astype(BF16)
        vh = qkv[:, 2 * D_MODEL + lo:2 * D_MODEL + hi]
        vo_ref[:, h, :] = vh
        vb_ref[h] = vh.astype(BF16)


def _qkv_prompt(x, g, w, cos, sneg, spos, kt_all, v_all, *, layer, batch, seq):
    tm = ROW_TILE
    nt = seq // tm
    row_spec = pl.BlockSpec((tm, D_MODEL), lambda b, t: (b * nt + t, 0))
    tab_spec = pl.BlockSpec((tm, V7X_LANES), lambda b, t: (t, 0))
    in_specs = [row_spec, _const_spec((1, D_MODEL)), _const_spec((D_MODEL, 3 * D_MODEL)),
                tab_spec, tab_spec, tab_spec]
    args = [x, g, w, cos, sneg, spos]
    aliases = {}
    if kt_all is not None:
        in_specs += [pl.BlockSpec(memory_space=pl.ANY)] * 2
        aliases = {len(args): 1, len(args) + 1: 2}
        args += [kt_all, v_all]
    return pl.pallas_call(
        _qkv_prompt_kernel,
        grid=(batch, nt),
        in_specs=in_specs,
        out_specs=[row_spec,
                   pl.BlockSpec((None, None, D_MODEL, tm), lambda b, t: (layer, b, 0, t)),
                   pl.BlockSpec((None, None, tm, N_HEADS, V_DIM), lambda b, t: (layer, b, t, 0, 0)),
                   pl.BlockSpec((None, D_MODEL, tm), lambda b, t: (b, 0, t)),
                   pl.BlockSpec((None, N_HEADS, tm, V_DIM), lambda b, t: (b, 0, t, 0))],
        out_shape=[jax.ShapeDtypeStruct((batch * seq, D_MODEL), BF16),
                   jax.ShapeDtypeStruct((N_ATTN_LAYERS, batch, D_MODEL, seq), F32),
                   jax.ShapeDtypeStruct((N_ATTN_LAYERS, batch, seq, N_HEADS, V_DIM), F32),
                   jax.ShapeDtypeStruct((batch, D_MODEL, seq), BF16),
                   jax.ShapeDtypeStruct((batch, N_HEADS, seq, V_DIM), BF16)],
        input_output_aliases=aliases,
        compiler_params=_cparams(("arbitrary", "arbitrary")),
        name="qkv_prompt",
    )(*args)


def _diff_lambda(lq1, lk1, lq2, lk2, lam_init):
    s1 = jnp.sum(lq1 * lk1, axis=-1, keepdims=True)
    s2 = jnp.sum(lq2 * lk2, axis=-1, keepdims=True)
    return jnp.exp(s1) - jnp.exp(s2) + lam_init


def _attn_kernel(lq1_ref, lk1_ref, lq2_ref, lk2_ref, sg_ref, q_ref, kt_ref, v_ref, o_ref,
                 q2_ref, m_ref, acc_ref, *, tile, lam_init):
    qi = pl.program_id(2)
    ki = pl.program_id(3)

    @pl.when(ki == 0)
    def _():
        q = q_ref[...]
        lane = lax.broadcasted_iota(jnp.int32, q.shape, 1)
        zero = jnp.zeros_like(q)
        q2_ref[0:tile, :] = jnp.where(lane < HEAD_DIM, q, zero)
        q2_ref[tile:2 * tile, :] = jnp.where(lane >= HEAD_DIM, q, zero)
        m_ref[...] = jnp.full_like(m_ref, NEG_INF)
        acc_ref[...] = jnp.zeros_like(acc_ref)

    def update(diagonal):
        s = jnp.dot(q2_ref[...], kt_ref[...], preferred_element_type=F32)
        if diagonal:
            row = lax.broadcasted_iota(jnp.int32, s.shape, 0)
            col = lax.broadcasted_iota(jnp.int32, s.shape, 1)
            s = jnp.where(col <= jnp.where(row >= tile, row - tile, row), s, NEG_INF)
        m_prev = m_ref[...]
        m_new = jnp.maximum(m_prev, jnp.max(s, axis=-1, keepdims=True))
        alpha = jnp.exp2(m_prev - m_new)
        p = jnp.exp2(s - jnp.tile(m_new, (1, tile // V7X_LANES))).astype(BF16)
        v1 = jnp.concatenate([v_ref[...], jnp.ones((tile, V7X_LANES), BF16)], axis=1)
        acc_ref[...] = jnp.tile(alpha, (1, 2)) * acc_ref[...] + jnp.dot(p, v1, preferred_element_type=F32)
        m_ref[...] = m_new

    @pl.when(ki < qi)
    def _():
        update(False)

    @pl.when(ki == qi)
    def _():
        update(True)
        lam = _diff_lambda(lq1_ref[...], lk1_ref[...], lq2_ref[...], lk2_ref[...], lam_init)
        o = acc_ref[:, 0:V_DIM] / acc_ref[:, V_DIM:2 * V_DIM]
        o = o[0:tile, :] - lam * o[tile:2 * tile, :]
        o_ref[...] = (_rms(o, sg_ref[...], SUBLN_EPS) * (1.0 - lam_init)).astype(o_ref.dtype)


def _attn_prompt(lq1, lk1, lq2, lk2, sg, q, ktb, vb, *, batch, seq, lam_init):
    tile = ATTN_TILE
    nq = seq // tile
    lam_spec = _const_spec((1, HEAD_DIM))
    q_spec = pl.BlockSpec((tile, V_DIM), lambda b, h, i, j: (b * nq + i, h))
    kt_spec = pl.BlockSpec((None, V_DIM, tile), lambda b, h, i, j: (b, h, jnp.minimum(j, i)))
    v_spec = pl.BlockSpec((None, None, tile, V_DIM), lambda b, h, i, j: (b, h, jnp.minimum(j, i), 0))
    return pl.pallas_call(
        functools.partial(_attn_kernel, tile=tile, lam_init=lam_init),
        grid=(batch, N_HEADS, nq, nq),
        in_specs=[lam_spec, lam_spec, lam_spec, lam_spec, _const_spec((1, V_DIM)), q_spec, kt_spec, v_spec],
        out_specs=q_spec,
        out_shape=jax.ShapeDtypeStruct((batch * seq, D_MODEL), BF16),
        scratch_shapes=[pltpu.VMEM((2 * tile, V_DIM), BF16), pltpu.VMEM((2 * tile, V7X_LANES), F32),
                        pltpu.VMEM((2 * tile, 2 * V_DIM), F32)],
        compiler_params=_cparams(("arbitrary",) * 4),
        name="attn_prompt",
    )(lq1, lk1, lq2, lk2, sg, q, ktb, vb)


def _attn_sample_kernel(pt_ref, lq1_ref, lk1_ref, lq2_ref, lk2_ref, sg_ref, e_ref, q_ref, kn_ref, vn_ref, *rest,
                        nt, lam_init):
    del pt_ref
    npp = PAGES_PER_STEP
    k_refs, v_refs = rest[:npp], rest[npp:2 * npp]
    o_ref = rest[2 * npp]
    qt_ref, qtb_ref, s_ref, p_ref, pn_ref, l_ref, acc_ref = rest[2 * npp + 1:]
    ph = pl.program_id(1)
    j = pl.program_id(2)
    nrow = nt * N_HEADS

    @pl.when((ph == 0) & (j == 0))
    def _():
        q = q_ref[0]
        sub = lax.broadcasted_iota(jnp.int32, (N_HEADS, D_MODEL), 0)
        col = lax.broadcasted_iota(jnp.int32, (N_HEADS, D_MODEL), 1)
        for c in range(2):
            keep = (col // V_DIM == sub) & ((col // HEAD_DIM) % 2 == c)
            for t in range(nt):
                r0 = c * nrow + t * N_HEADS
                qt_ref[r0:r0 + N_HEADS, :] = jnp.where(keep, jnp.broadcast_to(q[t:t + 1, :], (N_HEADS, D_MODEL)), 0.0)
        qtb_ref[...] = qt_ref[...].astype(BF16)

    @pl.when(ph == 0)
    def _():
        qtb = qtb_ref[...]
        for i in range(npp):
            kt = k_refs[i][...].astype(BF16)
            s_ref[j * npp + i] = jnp.dot(qtb, kt, preferred_element_type=F32)

    @pl.when((ph == 1) & (j == 0))
    def _():
        qt = qt_ref[...]
        kn = kn_ref[0]
        r = lax.broadcasted_iota(jnp.int32, (2 * nrow, 1), 0)
        t_row = (r // N_HEADS) % nt
        sn = []
        for t in range(nt):
            st = jnp.sum(qt * kn[t:t + 1, :], axis=-1, keepdims=True)
            sn.append(jnp.where(t <= t_row, st, NEG_INF))
        s_all = s_ref[...]
        m = jnp.max(jnp.max(s_all, axis=0), axis=-1, keepdims=True)
        for st in sn:
            m = jnp.maximum(m, st)
        p = jnp.exp2(s_all - m[None])
        l = jnp.sum(jnp.sum(p, axis=0), axis=-1, keepdims=True)
        p_ref[...] = p.astype(BF16).reshape(p_ref.shape)
        for t in range(nt):
            pt = jnp.exp2(sn[t] - m)
            pn_ref[t] = pt
            l = l + pt
        l_ref[...] = l
        acc_ref[...] = jnp.zeros_like(acc_ref)

    @pl.when(ph == 1)
    def _():
        rows = npp * 2 * nrow
        pe = jnp.dot(p_ref[pl.ds(pl.multiple_of(j * rows, rows), rows), :], e_ref[...],
                     preferred_element_type=F32)
        hh = lax.broadcasted_iota(jnp.int32, (N_HEADS, pe.shape[1]), 0)
        cc = lax.broadcasted_iota(jnp.int32, (N_HEADS, pe.shape[1]), 1)
        own = (hh == cc % N_HEADS)[None]
        pe = jnp.where(own, pe.reshape(rows // N_HEADS, N_HEADS, pe.shape[1]), 0.0).reshape(pe.shape).astype(BF16)
        acc = acc_ref[...]
        for i in range(npp):
            vb = v_refs[i][...].astype(BF16)
            acc = acc + jnp.dot(pe[i * 2 * nrow:(i + 1) * 2 * nrow, :], vb, preferred_element_type=F32)
        acc_ref[...] = acc

    @pl.when((ph == 1) & (j == pl.num_programs(2) - 1))
    def _():
        acc = acc_ref[...]
        for t in range(nt):
            vt = jnp.concatenate([vn_ref[0, t]] * (2 * nt), axis=0)
            acc = acc + pn_ref[t] * vt
        acc = acc / l_ref[...]
        lam = _diff_lambda(lq1_ref[...], lk1_ref[...], lq2_ref[...], lk2_ref[...], lam_init)
        o = acc[0:nrow, :] - lam * acc[nrow:2 * nrow, :]
        o = _rms(o, sg_ref[...], SUBLN_EPS) * (1.0 - lam_init)
        for t in range(nt):
            o_ref[0, t] = o[t * N_HEADS:(t + 1) * N_HEADS, :]


def _attn_sample(page_table, lq1, lk1, lq2, lk2, sg, q, kn, vn, cache_k, cache_v, *, layer, lam_init):
    nb, nt, _ = q.shape
    n_layers, n_pool = cache_k.shape[:2]
    n_pages = page_table.shape[1]
    npp = PAGES_PER_STEP
    nj = n_pages // npp
    nrow2 = 2 * nt * N_HEADS
    vrows = PAGE_SIZE * N_HEADS

    def k_map(i):
        return lambda b, ph, j, pt: (layer, pt[b, jnp.where(ph == 0, j, nj - 1) * npp + i], 0, 0)

    def v_map(i):
        return lambda b, ph, j, pt: (layer, pt[b, jnp.where(ph == 1, j, 0) * npp + i], 0, 0)

    const2 = lambda shape: pl.BlockSpec(shape, lambda b, ph, j, pt: (0, 0))
    seq_spec = pl.BlockSpec((1, nt, D_MODEL), lambda b, ph, j, pt: (b, 0, 0))
    head_spec = pl.BlockSpec((1, nt, N_HEADS, V_DIM), lambda b, ph, j, pt: (b, 0, 0, 0))
    lam_spec = const2((1, HEAD_DIM))
    grid_spec = pltpu.PrefetchScalarGridSpec(
        num_scalar_prefetch=1,
        grid=(nb, 2, nj),
        in_specs=[lam_spec, lam_spec, lam_spec, lam_spec, const2((1, V_DIM)), const2((PAGE_SIZE, vrows)),
                  seq_spec, seq_spec, head_spec]
                 + [pl.BlockSpec((None, None, D_MODEL, PAGE_SIZE), k_map(i)) for i in range(npp)]
                 + [pl.BlockSpec((None, None, vrows, V_DIM), v_map(i)) for i in range(npp)],
        out_specs=head_spec,
        scratch_shapes=[pltpu.VMEM((nrow2, D_MODEL), F32), pltpu.VMEM((nrow2, D_MODEL), BF16),
                        pltpu.VMEM((n_pages, nrow2, PAGE_SIZE), F32),
                        pltpu.VMEM((n_pages * nrow2, PAGE_SIZE), BF16),
                        pltpu.VMEM((nt, nrow2, 1), F32), pltpu.VMEM((nrow2, 1), F32),
                        pltpu.VMEM((nrow2, V_DIM), F32)],
    )
    ck = jnp.transpose(cache_k, (0, 1, 3, 4, 5, 2)).reshape(n_layers, n_pool, D_MODEL, PAGE_SIZE)
    cv = cache_v.reshape(n_layers, n_pool, vrows, V_DIM)
    spread = (jnp.arange(vrows)[None, :] // N_HEADS == jnp.arange(PAGE_SIZE)[:, None]).astype(BF16)
    return pl.pallas_call(
        functools.partial(_attn_sample_kernel, nt=nt, lam_init=lam_init),
        grid_spec=grid_spec,
        out_shape=jax.ShapeDtypeStruct((nb, nt, N_HEADS, V_DIM), F32),
        compiler_params=_cparams(("arbitrary",) * 3),
        name="attn_sample",
    )(page_table, lq1, lk1, lq2, lk2, sg, spread, q, kn, vn, *([ck] * npp), *([cv] * npp))


def _post_kernel(x_ref, y_ref, wp_ref, g_ref, wi_ref, wo_ref, gf_ref, o_ref, *, final):
    x = x_ref[...] + jnp.dot(y_ref[...].astype(BF16), wp_ref[...], preferred_element_type=F32)
    n = _rms(x, g_ref[...], NORM_EPS).astype(BF16)
    out = x
    for c in range(N_FF_CHUNKS):
        lo, hi = c * FF_CHUNK, (c + 1) * FF_CHUNK
        gc = jnp.dot(n, wi_ref[:, lo:hi], preferred_element_type=F32)
        uc = jnp.dot(n, wi_ref[:, D_FF + lo:D_FF + hi], preferred_element_type=F32)
        act = (gc * jax.nn.sigmoid(gc) * uc).astype(BF16)
        out = out + jnp.dot(act, wo_ref[lo:hi, :], preferred_element_type=F32)
    if final:
        out = _rms(out, gf_ref[...], NORM_EPS)
    o_ref[...] = out


def _post(x, y, wp, g, wi, wo, gf, *, tm, final):
    m = x.shape[0]
    row_spec = pl.BlockSpec((tm, D_MODEL), lambda i: (i, 0))
    return pl.pallas_call(
        functools.partial(_post_kernel, final=final),
        grid=(m // tm,),
        in_specs=[row_spec, row_spec, _const_spec((D_MODEL, D_MODEL)), _const_spec((1, D_MODEL)),
                  _const_spec((D_MODEL, 2 * D_FF)), _const_spec((D_FF, D_MODEL)), _const_spec((1, D_MODEL))],
        out_specs=row_spec,
        out_shape=jax.ShapeDtypeStruct((m, D_MODEL), F32),
        compiler_params=_cparams(("arbitrary",)),
        name="post",
    )(x, y, wp, g, wi, wo, gf)


def kernel(x_prompt, x_sample, cache_k, cache_v, page_table, state_conv, state_h, norm_mix, w_rec_in, conv_w,
           conv_b, gate_x_w, gate_x_b, gate_a_w, gate_a_b, lru_lambda, w_rec_out, w_qkv, lambda_q1, lambda_k1,
           lambda_q2, lambda_k2, subln_g, w_attn_out, norm_ffn, w_ffn_in, w_ffn_out, norm_final):
    B, T, _ = x_prompt.shape
    DB, Tn, _ = x_sample.shape
    past = page_table.shape[1] * PAGE_SIZE
    row = lambda a: a.reshape(1, -1)

    hp = x_prompt.reshape(B * T, D_MODEL)
    hs = x_sample.reshape(DB * Tn, D_MODEL)
    tabs_p = _rope_tables(jnp.arange(T, dtype=F32))
    tabs_s = _rope_tables(jnp.tile(past + jnp.arange(Tn, dtype=F32), DB))

    kt_all = v_all = None
    pconv, ph = [], []
    sk, sv, sconv, sh = [], [], [], []
    for i in range(DEPTH):
        g_mix = row(norm_mix[i])
        if i % N_MIXERS == 0:
            r = i // N_MIXERS
            w = (w_rec_in[r].astype(BF16), conv_w[r], row(conv_b[r]),
                 gate_x_w[r].astype(BF16), row(gate_x_b[r]), gate_a_w[r].astype(BF16), row(gate_a_b[r]),
                 row(lru_lambda[r]))
            yp, c_p, h_p = _rec_prompt(hp, g_mix, *w, batch=B, seq=T)
            to_tm = lambda t, n: t.reshape(DB, n, D_RNN).swapaxes(0, 1).reshape(n * DB, D_RNN)
            from_tm = lambda t, n: t.reshape(n, DB, D_RNN).swapaxes(0, 1)
            ys, c_s, h_s = _rec_sample(to_tm(hs, Tn), to_tm(state_conv[r], CONV_W - 1), state_h[r], g_mix, *w,
                                       nb=DB, nt=Tn)
            ys = from_tm(ys, Tn).reshape(DB * Tn, D_RNN)
            pconv.append(c_p); ph.append(h_p.reshape(B, D_RNN))
            sconv.append(from_tm(c_s, CONV_W - 1)); sh.append(h_s)
            w_proj = w_rec_out[r].astype(BF16)
        else:
            a = i // N_MIXERS
            lam_init = 0.8 - 0.6 * math.exp(-0.3 * i)
            lams = (row(lambda_q1[a]), row(lambda_k1[a]), row(lambda_q2[a]), row(lambda_k2[a]))
            wq = w_qkv[a].astype(BF16)
            q_p, kt_all, v_all, ktb, vb = _qkv_prompt(hp, g_mix, wq, *tabs_p, kt_all, v_all,
                                                      layer=a, batch=B, seq=T)
            yp = _attn_prompt(*lams, row(subln_g[a]), q_p, ktb, vb, batch=B, seq=T, lam_init=lam_init)
            q_s, k_s, v_s = _qkv_sample(hs, g_mix, wq, *tabs_s)
            seq3 = lambda t: t.reshape(DB, Tn, D_MODEL)
            ys = _attn_sample(page_table, *lams, row(subln_g[a]), seq3(q_s), seq3(k_s),
                              v_s.reshape(DB, Tn, N_HEADS, V_DIM), cache_k, cache_v,
                              layer=a, lam_init=lam_init).reshape(DB * Tn, D_MODEL)
            sk.append(k_s.reshape(DB, Tn, N_HEADS, 2, HEAD_DIM)); sv.append(v_s.reshape(DB, Tn, N_HEADS, V_DIM))
            w_proj = w_attn_out[a].astype(BF16)

        final = i == DEPTH - 1
        post_w = (w_proj, row(norm_ffn[i]), w_ffn_in[i].astype(BF16), w_ffn_out[i].astype(BF16), row(norm_final))
        hp = _post(hp, yp, *post_w, tm=ROW_TILE, final=final)
        hs = _post(hs, ys, *post_w, tm=DB * Tn, final=final)

    prompt_k = jnp.transpose(kt_all.reshape(N_ATTN_LAYERS, B, N_HEADS, 2, HEAD_DIM, T), (0, 1, 5, 2, 3, 4))
    return (hp.reshape(B, T, D_MODEL), hs.reshape(DB, Tn, D_MODEL),
            prompt_k, v_all, jnp.stack(pconv), jnp.stack(ph),
            jnp.stack(sk), jnp.stack(sv), jnp.stack(sconv), jnp.stack(sh))
```

```python
import functools
import math

import jax
import jax.numpy as jnp
from jax import lax
from jax.experimental import pallas as pl
from jax.experimental.pallas import tpu as pltpu

F32 = jnp.float32
BF16 = jnp.bfloat16

D_MODEL = 1024
DEPTH = 4
N_MIXERS = 2
N_ATTN_LAYERS = DEPTH // N_MIXERS
D_RNN = D_MODEL
N_RNN_BLOCKS = 4
RNN_BLOCK = D_RNN // N_RNN_BLOCKS
CONV_W = 4
LRU_C = 8.0
HEAD_DIM = 64
N_HEADS = D_MODEL // (2 * HEAD_DIM)
V_DIM = 2 * HEAD_DIM
ROT_DIM = HEAD_DIM // 4
ROPE_THETA = 500000.0
D_FF = -(-8 * D_MODEL // (3 * 256)) * 256
PAGE_SIZE = 128
NORM_EPS = 1e-6
SUBLN_EPS = 1e-5
NEG_INF = -1e30
QK_SCALE = HEAD_DIM ** -0.5 * math.log2(math.e)

V7X_VMEM_BYTES = 64 * 1024 * 1024
V7X_SUBLANES = 8
V7X_LANES = 128
V7X_MXU_DIM = 256

ROW_TILE = 512
FF_CHUNK = V7X_MXU_DIM
N_FF_CHUNKS = D_FF // FF_CHUNK
ATTN_TILE = 512
PAGES_PER_STEP = 8
VMEM_LIMIT = V7X_VMEM_BYTES - 8 * 1024 * 1024


def _cparams(semantics):
    return pltpu.CompilerParams(dimension_semantics=semantics, vmem_limit_bytes=VMEM_LIMIT)


def _const_spec(shape):
    nd = len(shape)
    return pl.BlockSpec(shape, lambda *_: (0,) * nd, pipeline_mode=pl.Buffered(1))


def _rms(x, g, eps):
    ms = jnp.mean(x * x, axis=-1, keepdims=True)
    return x * lax.rsqrt(ms + eps) * g


def _log_sigmoid(x):
    return jnp.minimum(x, 0.0) - jnp.log1p(jnp.exp(-jnp.abs(x)))


def _lru_coeffs(xc, wx_ref, bx, wa_ref, ba, lam):
    xcb = xc.astype(BF16)

    def gate(w_ref, b):
        z = [jnp.dot(xcb[:, n * RNN_BLOCK:(n + 1) * RNN_BLOCK], w_ref[n], preferred_element_type=F32)
             for n in range(N_RNN_BLOCKS)]
        return jax.nn.sigmoid(jnp.concatenate(z, axis=-1) + b)

    gx = gate(wx_ref, bx)
    ga = gate(wa_ref, ba)
    log_a = ga * (LRU_C * _log_sigmoid(lam))
    a = jnp.exp(log_a)
    mult = jnp.sqrt(jnp.maximum(-jnp.tanh(log_a) * (a * a + 1.0), 0.0))
    return a, mult * gx * xc


def _rec_prompt_kernel(x_ref, g_ref, win_ref, cw_ref, cb_ref, wx_ref, bx_ref, wa_ref, ba_ref, lam_ref,
                       y_ref, conv_ref, h_ref,
                       xe_ref, a_ref, u_ref, hs_ref, hc_ref, *, tm):
    ti = pl.program_id(1)
    pad = V7X_SUBLANES

    @pl.when(ti == 0)
    def _():
        xe_ref[...] = jnp.zeros_like(xe_ref)
        hc_ref[...] = jnp.zeros_like(hc_ref)

    n = _rms(x_ref[...], g_ref[...], NORM_EPS).astype(BF16)
    y = jnp.dot(n, win_ref[...], preferred_element_type=F32)
    xr = y[:, :D_RNN]
    gate = y[:, D_RNN:]

    cw = cw_ref[...]
    prev = xe_ref[...]
    row = lax.broadcasted_iota(jnp.int32, (pad, D_RNN), 0)
    xc = cb_ref[...]
    for k in range(CONV_W):
        shift = CONV_W - 1 - k
        if shift == 0:
            xs = xr
        else:
            rolled = pltpu.roll(xr, shift, 0)
            head = jnp.where(row < shift, pltpu.roll(prev, shift, 0), rolled[0:pad, :])
            xs = jnp.concatenate([head, rolled[pad:, :]], axis=0)
        xc = xc + xs * cw[k:k + 1, :]
    xe_ref[...] = xr[tm - pad:tm, :]

    a, u = _lru_coeffs(xc, wx_ref, bx_ref[...], wa_ref, ba_ref[...], lam_ref[...])
    a_ref[...] = a
    u_ref[...] = u

    def group(gidx, h):
        r0 = pl.multiple_of(gidx * pad, pad)
        ag = a_ref[pl.ds(r0, pad), :]
        ug = u_ref[pl.ds(r0, pad), :]
        for k in (1, 2, 4):
            m = row >= k
            a_s = jnp.where(m, pltpu.roll(ag, k, 0), 1.0)
            u_s = jnp.where(m, pltpu.roll(ug, k, 0), 0.0)
            ug = ug + ag * u_s
            ag = ag * a_s
        hrows = ag * h + ug
        hs_ref[pl.ds(r0, pad), :] = hrows
        return hrows[pad - 1:pad, :]

    h_last = lax.fori_loop(0, tm // pad, group, hc_ref[...])
    hc_ref[...] = h_last

    y_ref[...] = (hs_ref[...] * jax.nn.gelu(gate)).astype(y_ref.dtype)

    @pl.when(ti == pl.num_programs(1) - 1)
    def _():
        conv_ref[0] = xe_ref[pad - (CONV_W - 1):pad, :]
        h_ref[0] = h_last


def _rec_prompt(x, g, win, cw, cb, wx, bx, wa, ba, lam, *, batch, seq):
    tm = ROW_TILE
    nt = seq // tm
    row_spec = pl.BlockSpec((tm, D_MODEL), lambda b, t: (b * nt + t, 0))
    return pl.pallas_call(
        functools.partial(_rec_prompt_kernel, tm=tm),
        grid=(batch, nt),
        in_specs=[row_spec, _const_spec((1, D_MODEL)), _const_spec((D_MODEL, 2 * D_RNN)),
                  _const_spec((CONV_W, D_RNN)), _const_spec((1, D_RNN)),
                  _const_spec((N_RNN_BLOCKS, RNN_BLOCK, RNN_BLOCK)), _const_spec((1, D_RNN)),
                  _const_spec((N_RNN_BLOCKS, RNN_BLOCK, RNN_BLOCK)), _const_spec((1, D_RNN)),
                  _const_spec((1, D_RNN))],
        out_specs=[row_spec,
                   pl.BlockSpec((1, CONV_W - 1, D_RNN), lambda b, t: (b, 0, 0)),
                   pl.BlockSpec((1, 1, D_RNN), lambda b, t: (b, 0, 0))],
        out_shape=[jax.ShapeDtypeStruct((batch * seq, D_RNN), BF16),
                   jax.ShapeDtypeStruct((batch, CONV_W - 1, D_RNN), F32),
                   jax.ShapeDtypeStruct((batch, 1, D_RNN), F32)],
        scratch_shapes=[pltpu.VMEM((V7X_SUBLANES, D_RNN), F32),
                        pltpu.VMEM((tm, D_RNN), F32), pltpu.VMEM((tm, D_RNN), F32),
                        pltpu.VMEM((tm, D_RNN), F32), pltpu.VMEM((1, D_RNN), F32)],
        compiler_params=_cparams(("arbitrary", "arbitrary")),
        name="rec_prompt",
    )(x, g, win, cw, cb, wx, bx, wa, ba, lam)


def _rec_sample_kernel(x_ref, sc_ref, h0_ref, g_ref, win_ref, cw_ref, cb_ref, wx_ref, bx_ref, wa_ref, ba_ref,
                       lam_ref, y_ref, conv_ref, h_ref, *, nb, nt):
    n = _rms(x_ref[...], g_ref[...], NORM_EPS).astype(BF16)
    y = jnp.dot(n, win_ref[...], preferred_element_type=F32)
    gate = y[:, D_RNN:]

    nbuf = CONV_W - 1
    xe = [sc_ref[k * nb:(k + 1) * nb, :] for k in range(nbuf)]
    xe += [y[t * nb:(t + 1) * nb, :D_RNN] for t in range(nt)]
    cw = cw_ref[...]
    slabs = []
    for t in range(nt):
        xc = cb_ref[...]
        for k in range(CONV_W):
            xc = xc + xe[t + k] * cw[k:k + 1, :]
        slabs.append(xc)
    xc = jnp.concatenate(slabs, axis=0)

    a, u = _lru_coeffs(xc, wx_ref, bx_ref[...], wa_ref, ba_ref[...], lam_ref[...])
    h = h0_ref[...]
    for t in range(nt):
        rows = slice(t * nb, (t + 1) * nb)
        h = a[rows] * h + u[rows]
        y_ref[rows, :] = h * jax.nn.gelu(gate[rows])
    h_ref[...] = h
    for k in range(nbuf):
        conv_ref[k * nb:(k + 1) * nb, :] = xe[len(xe) - nbuf + k]


def _rec_sample(x, sc, h0, g, win, cw, cb, wx, bx, wa, ba, lam, *, nb, nt):
    m = nb * nt
    nbuf = CONV_W - 1
    return pl.pallas_call(
        functools.partial(_rec_sample_kernel, nb=nb, nt=nt),
        out_shape=[jax.ShapeDtypeStruct((m, D_RNN), F32),
                   jax.ShapeDtypeStruct((nb * nbuf, D_RNN), F32),
                   jax.ShapeDtypeStruct((nb, D_RNN), F32)],
        compiler_params=pltpu.CompilerParams(vmem_limit_bytes=VMEM_LIMIT),
        name="rec_sample",
    )(x, sc, h0, g, win, cw, cb, wx, bx, wa, ba, lam)


def _rope(xh, c, sneg, spos):
    half = ROT_DIM // 2
    return xh * c + pltpu.roll(xh, V7X_LANES - half, 1) * sneg + pltpu.roll(xh, half, 1) * spos


def _rope_tables(pos):
    half = ROT_DIM // 2
    inv = jnp.power(ROPE_THETA, -jnp.arange(half, dtype=F32) * 2.0 / ROT_DIM)
    ang = pos[:, None] * inv[None, :]
    cos, sin = jnp.cos(ang), jnp.sin(ang)
    n = pos.shape[0]
    ones = jnp.ones((n, HEAD_DIM - ROT_DIM), F32)
    zeros = jnp.zeros((n, HEAD_DIM - ROT_DIM), F32)
    zh = jnp.zeros((n, half), F32)
    c = jnp.concatenate([cos, cos, ones], axis=-1)
    sneg = jnp.concatenate([-sin, zh, zeros], axis=-1)
    spos = jnp.concatenate([zh, sin, zeros], axis=-1)
    rep = V7X_LANES // HEAD_DIM
    return tuple(jnp.tile(t, (1, rep)) for t in (c, sneg, spos))


def _qkv_sample_kernel(x_ref, g_ref, w_ref, cos_ref, sneg_ref, spos_ref, q_ref, k_ref, v_ref):
    n = _rms(x_ref[...], g_ref[...], NORM_EPS).astype(BF16)
    qkv = jnp.dot(n, w_ref[...], preferred_element_type=F32)
    tabs = (cos_ref[...], sneg_ref[...], spos_ref[...])
    for h in range(N_HEADS):
        lo, hi = h * V_DIM, (h + 1) * V_DIM
        q_ref[:, lo:hi] = _rope(qkv[:, lo:hi], *tabs) * QK_SCALE
        k_ref[:, lo:hi] = _rope(qkv[:, D_MODEL + lo:D_MODEL + hi], *tabs)
    v_ref[...] = qkv[:, 2 * D_MODEL:]


def _qkv_sample(x, g, w, cos, sneg, spos):
    out = jax.ShapeDtypeStruct((x.shape[0], D_MODEL), F32)
    return pl.pallas_call(
        _qkv_sample_kernel,
        out_shape=[out, out, out],
        compiler_params=pltpu.CompilerParams(vmem_limit_bytes=VMEM_LIMIT),
        name="qkv_sample",
    )(x, g, w, cos, sneg, spos)


def _qkv_prompt_kernel(x_ref, g_ref, w_ref, cos_ref, sneg_ref, spos_ref, *rest):
    q_ref, kt_ref, vo_ref, ktb_ref, vb_ref = rest[-5:]
    n = _rms(x_ref[...], g_ref[...], NORM_EPS).astype(BF16)
    qkv = jnp.dot(n, w_ref[...], preferred_element_type=F32)
    tabs = (cos_ref[...], sneg_ref[...], spos_ref[...])
    for h in range(N_HEADS):
        lo, hi = h * V_DIM, (h + 1) * V_DIM
        q_ref[h] = (_rope(qkv[:, lo:hi], *tabs) * QK_SCALE).astype(q_ref.dtype)
        kt = _rope(qkv[:, D_MODEL + lo:D_MODEL + hi], *tabs).T
        kt_ref[lo:hi, :] = kt
        ktb_ref[lo:hi, :] = kt.astype(BF16)
        vh = qkv[:, 2 * D_MODEL + lo:2 * D_MODEL + hi]
        vo_ref[:, h, :] = vh
        vb_ref[h] = vh.astype(BF16)


def _qkv_prompt(x, g, w, cos, sneg, spos, kt_all, v_all, *, layer, batch, seq):
    tm = ROW_TILE
    nt = seq // tm
    row_spec = pl.BlockSpec((tm, D_MODEL), lambda b, t: (b * nt + t, 0))
    tab_spec = pl.BlockSpec((tm, V7X_LANES), lambda b, t: (t, 0))
    in_specs = [row_spec, _const_spec((1, D_MODEL)), _const_spec((D_MODEL, 3 * D_MODEL)),
                tab_spec, tab_spec, tab_spec]
    args = [x, g, w, cos, sneg, spos]
    aliases = {}
    if kt_all is not None:
        in_specs += [pl.BlockSpec(memory_space=pl.ANY)] * 2
        aliases = {len(args): 1, len(args) + 1: 2}
        args += [kt_all, v_all]
    return pl.pallas_call(
        _qkv_prompt_kernel,
        grid=(batch, nt),
        in_specs=in_specs,
        out_specs=[pl.BlockSpec((None, N_HEADS, tm, V_DIM), lambda b, t: (b, 0, t, 0)),
                   pl.BlockSpec((None, None, D_MODEL, tm), lambda b, t: (layer, b, 0, t)),
                   pl.BlockSpec((None, None, tm, N_HEADS, V_DIM), lambda b, t: (layer, b, t, 0, 0)),
                   pl.BlockSpec((None, D_MODEL, tm), lambda b, t: (b, 0, t)),
                   pl.BlockSpec((None, N_HEADS, tm, V_DIM), lambda b, t: (b, 0, t, 0))],
        out_shape=[jax.ShapeDtypeStruct((batch, N_HEADS, seq, V_DIM), BF16),
                   jax.ShapeDtypeStruct((N_ATTN_LAYERS, batch, D_MODEL, seq), F32),
                   jax.ShapeDtypeStruct((N_ATTN_LAYERS, batch, seq, N_HEADS, V_DIM), F32),
                   jax.ShapeDtypeStruct((batch, D_MODEL, seq), BF16),
                   jax.ShapeDtypeStruct((batch, N_HEADS, seq, V_DIM), BF16)],
        input_output_aliases=aliases,
        compiler_params=_cparams(("arbitrary", "arbitrary")),
        name="qkv_prompt",
    )(*args)


def _diff_lambda(lq1, lk1, lq2, lk2, lam_init):
    s1 = jnp.sum(lq1 * lk1, axis=-1, keepdims=True)
    s2 = jnp.sum(lq2 * lk2, axis=-1, keepdims=True)
    return jnp.exp(s1) - jnp.exp(s2) + lam_init


def _attn_kernel(qi_ref, ki_ref, lq1_ref, lk1_ref, lq2_ref, lk2_ref, sg_ref, q_ref, kt_ref, v_ref, o_ref,
                 q2_ref, m_ref, acc_ref, oh_ref, *, tile, lam_init):
    step = pl.program_id(1)
    qi = qi_ref[step]
    ki = ki_ref[step]

    @pl.when(ki == 0)
    def _():
        q = q_ref[...]
        lane = lax.broadcasted_iota(jnp.int32, q.shape, 2)
        zero = jnp.zeros_like(q)
        q2_ref[:, 0:tile, :] = jnp.where(lane < HEAD_DIM, q, zero)
        q2_ref[:, tile:2 * tile, :] = jnp.where(lane >= HEAD_DIM, q, zero)
        m_ref[...] = jnp.full_like(m_ref, NEG_INF)
        acc_ref[...] = jnp.zeros_like(acc_ref)

    def update(h, diagonal):
        kt = kt_ref[pl.ds(pl.multiple_of(h * V_DIM, V_DIM), V_DIM), :]
        s = jnp.dot(q2_ref[h], kt, preferred_element_type=F32)
        if diagonal:
            row = lax.broadcasted_iota(jnp.int32, s.shape, 0)
            col = lax.broadcasted_iota(jnp.int32, s.shape, 1)
            s = jnp.where(col <= jnp.where(row >= tile, row - tile, row), s, NEG_INF)
        m_prev = m_ref[h]
        m_new = jnp.maximum(m_prev, jnp.max(s, axis=-1, keepdims=True))
        alpha = jnp.exp2(m_prev - m_new)
        p = jnp.exp2(s - jnp.tile(m_new, (1, tile // V7X_LANES))).astype(BF16)
        v1 = jnp.concatenate([v_ref[h], jnp.ones((tile, V7X_LANES), BF16)], axis=1)
        acc_ref[h] = jnp.tile(alpha, (1, 2)) * acc_ref[h] + jnp.dot(p, v1, preferred_element_type=F32)
        m_ref[h] = m_new

    @pl.when(ki < qi)
    def _():
        def head(h, carry):
            update(h, False)
            return carry
        lax.fori_loop(0, N_HEADS, head, 0)

    @pl.when(ki == qi)
    def _():
        lam = _diff_lambda(lq1_ref[...], lk1_ref[...], lq2_ref[...], lk2_ref[...], lam_init)

        def head(h, carry):
            update(h, True)
            acc = acc_ref[h]
            o = acc[:, 0:V_DIM] / acc[:, V_DIM:2 * V_DIM]
            o = o[0:tile, :] - lam * o[tile:2 * tile, :]
            oh_ref[h] = (_rms(o, sg_ref[...], SUBLN_EPS) * (1.0 - lam_init)).astype(oh_ref.dtype)
            return carry
        lax.fori_loop(0, N_HEADS, head, 0)
        for h in range(N_HEADS):
            o_ref[:, h * V_DIM:(h + 1) * V_DIM] = oh_ref[h]


def _attn_prompt(lq1, lk1, lq2, lk2, sg, q, ktb, vb, *, batch, seq, lam_init):
    tile = ATTN_TILE
    nq = seq // tile
    pairs = [(i, j) for i in range(nq) for j in range(i + 1)]
    qi_tab = jnp.array([i for i, _ in pairs], jnp.int32)
    ki_tab = jnp.array([j for _, j in pairs], jnp.int32)
    const2 = lambda shape: pl.BlockSpec(shape, lambda b, s, qt, kt: (0, 0))
    lam_spec = const2((1, HEAD_DIM))
    q_spec = pl.BlockSpec((None, N_HEADS, tile, V_DIM), lambda b, s, qt, kt: (b, 0, qt[s], 0))
    kt_spec = pl.BlockSpec((None, D_MODEL, tile), lambda b, s, qt, kt: (b, 0, kt[s]))
    v_spec = pl.BlockSpec((None, N_HEADS, tile, V_DIM), lambda b, s, qt, kt: (b, 0, kt[s], 0))
    grid_spec = pltpu.PrefetchScalarGridSpec(
        num_scalar_prefetch=2,
        grid=(batch, len(pairs)),
        in_specs=[lam_spec, lam_spec, lam_spec, lam_spec, const2((1, V_DIM)), q_spec, kt_spec, v_spec],
        out_specs=pl.BlockSpec((tile, D_MODEL), lambda b, s, qt, kt: (b * nq + qt[s], 0)),
        scratch_shapes=[pltpu.VMEM((N_HEADS, 2 * tile, V_DIM), BF16),
                        pltpu.VMEM((N_HEADS, 2 * tile, V7X_LANES), F32),
                        pltpu.VMEM((N_HEADS, 2 * tile, 2 * V_DIM), F32),
                        pltpu.VMEM((N_HEADS, tile, V_DIM), BF16)],
    )
    return pl.pallas_call(
        functools.partial(_attn_kernel, tile=tile, lam_init=lam_init),
        grid_spec=grid_spec,
        out_shape=jax.ShapeDtypeStruct((batch * seq, D_MODEL), BF16),
        compiler_params=_cparams(("arbitrary", "arbitrary")),
        name="attn_prompt",
    )(qi_tab, ki_tab, lq1, lk1, lq2, lk2, sg, q, ktb, vb)


def _attn_sample_kernel(pt_ref, lq1_ref, lk1_ref, lq2_ref, lk2_ref, sg_ref, e_ref, q_ref, kn_ref, vn_ref, *rest,
                        nt, lam_init):
    del pt_ref
    npp = PAGES_PER_STEP
    k_refs, v_refs = rest[:npp], rest[npp:2 * npp]
    o_ref = rest[2 * npp]
    qt_ref, qtb_ref, m_ref, l_ref, acc_ref = rest[2 * npp + 1:]
    j = pl.program_id(1)
    nrow = nt * N_HEADS

    @pl.when(j == 0)
    def _():
        q = q_ref[0]
        sub = lax.broadcasted_iota(jnp.int32, (N_HEADS, D_MODEL), 0)
        col = lax.broadcasted_iota(jnp.int32, (N_HEADS, D_MODEL), 1)
        for c in range(2):
            keep = (col // V_DIM == sub) & ((col // HEAD_DIM) % 2 == c)
            for t in range(nt):
                r0 = c * nrow + t * N_HEADS
                qt_ref[r0:r0 + N_HEADS, :] = jnp.where(keep, jnp.broadcast_to(q[t:t + 1, :], (N_HEADS, D_MODEL)), 0.0)
        qtb_ref[...] = qt_ref[...].astype(BF16)

        qt = qt_ref[...]
        kn = kn_ref[0]
        r = lax.broadcasted_iota(jnp.int32, (2 * nrow, 1), 0)
        t_row = (r // N_HEADS) % nt
        sn = []
        for t in range(nt):
            st = jnp.sum(qt * kn[t:t + 1, :], axis=-1, keepdims=True)
            sn.append(jnp.where(t <= t_row, st, NEG_INF))
        m = sn[0]
        for st in sn[1:]:
            m = jnp.maximum(m, st)
        l = jnp.zeros_like(m)
        acc = jnp.zeros(acc_ref.shape, F32)
        for t in range(nt):
            pt = jnp.exp2(sn[t] - m)
            l = l + pt
            vt = jnp.concatenate([vn_ref[0, t]] * (2 * nt), axis=0)
            acc = acc + pt * vt
        m_ref[...] = m
        l_ref[...] = l
        acc_ref[...] = acc

    qtb = qtb_ref[...]
    s = jnp.concatenate([jnp.dot(qtb, k_refs[i][...].astype(BF16), preferred_element_type=F32)
                         for i in range(npp)], axis=1)
    m_prev = m_ref[...]
    m_new = jnp.maximum(m_prev, jnp.max(s, axis=-1, keepdims=True))
    alpha = jnp.exp2(m_prev - m_new)
    p = jnp.exp2(s - m_new)
    l_ref[...] = alpha * l_ref[...] + jnp.sum(p, axis=-1, keepdims=True)
    m_ref[...] = m_new
    pb = p.astype(BF16)
    pstack = jnp.concatenate([pb[:, i * PAGE_SIZE:(i + 1) * PAGE_SIZE] for i in range(npp)], axis=0)
    pe = jnp.dot(pstack, e_ref[...], preferred_element_type=F32)
    hh = lax.broadcasted_iota(jnp.int32, (N_HEADS, pe.shape[1]), 0)
    cc = lax.broadcasted_iota(jnp.int32, (N_HEADS, pe.shape[1]), 1)
    own = (hh == cc % N_HEADS)[None]
    pe = jnp.where(own, pe.reshape(pe.shape[0] // N_HEADS, N_HEADS, pe.shape[1]), 0.0).reshape(pe.shape)
    pe = pe.astype(BF16)
    acc = alpha * acc_ref[...]
    for i in range(npp):
        vb = v_refs[i][...].astype(BF16)
        acc = acc + jnp.dot(pe[i * 2 * nrow:(i + 1) * 2 * nrow, :], vb, preferred_element_type=F32)
    acc_ref[...] = acc

    @pl.when(j == pl.num_programs(1) - 1)
    def _():
        acc = acc_ref[...] / l_ref[...]
        lam = _diff_lambda(lq1_ref[...], lk1_ref[...], lq2_ref[...], lk2_ref[...], lam_init)
        o = acc[0:nrow, :] - lam * acc[nrow:2 * nrow, :]
        o = _rms(o, sg_ref[...], SUBLN_EPS) * (1.0 - lam_init)
        for t in range(nt):
            o_ref[0, t] = o[t * N_HEADS:(t + 1) * N_HEADS, :]


def _attn_sample(page_table, lq1, lk1, lq2, lk2, sg, q, kn, vn, cache_k, cache_v, *, layer, lam_init):
    nb, nt, _ = q.shape
    n_layers, n_pool = cache_k.shape[:2]
    n_pages = page_table.shape[1]
    npp = PAGES_PER_STEP
    nj = n_pages // npp
    nrow2 = 2 * nt * N_HEADS
    vrows = PAGE_SIZE * N_HEADS

    def page_map(i):
        return lambda b, j, pt: (layer, pt[b, j * npp + i], 0, 0)

    const2 = lambda shape: pl.BlockSpec(shape, lambda b, j, pt: (0, 0))
    seq_spec = pl.BlockSpec((1, nt, D_MODEL), lambda b, j, pt: (b, 0, 0))
    head_spec = pl.BlockSpec((1, nt, N_HEADS, V_DIM), lambda b, j, pt: (b, 0, 0, 0))
    lam_spec = const2((1, HEAD_DIM))
    grid_spec = pltpu.PrefetchScalarGridSpec(
        num_scalar_prefetch=1,
        grid=(nb, nj),
        in_specs=[lam_spec, lam_spec, lam_spec, lam_spec, const2((1, V_DIM)), const2((PAGE_SIZE, vrows)),
                  seq_spec, seq_spec, head_spec]
                 + [pl.BlockSpec((None, None, D_MODEL, PAGE_SIZE), page_map(i)) for i in range(npp)]
                 + [pl.BlockSpec((None, None, vrows, V_DIM), page_map(i)) for i in range(npp)],
        out_specs=head_spec,
        scratch_shapes=[pltpu.VMEM((nrow2, D_MODEL), F32), pltpu.VMEM((nrow2, D_MODEL), BF16),
                        pltpu.VMEM((nrow2, 1), F32), pltpu.VMEM((nrow2, 1), F32),
                        pltpu.VMEM((nrow2, V_DIM), F32)],
    )
    ck = jnp.transpose(cache_k, (0, 1, 3, 4, 5, 2)).reshape(n_layers, n_pool, D_MODEL, PAGE_SIZE)
    cv = cache_v.reshape(n_layers, n_pool, vrows, V_DIM)
    spread = (jnp.arange(vrows)[None, :] // N_HEADS == jnp.arange(PAGE_SIZE)[:, None]).astype(BF16)
    return pl.pallas_call(
        functools.partial(_attn_sample_kernel, nt=nt, lam_init=lam_init),
        grid_spec=grid_spec,
        out_shape=jax.ShapeDtypeStruct((nb, nt, N_HEADS, V_DIM), F32),
        compiler_params=_cparams(("arbitrary", "arbitrary")),
        name="attn_sample",
    )(page_table, lq1, lk1, lq2, lk2, sg, spread, q, kn, vn, *([ck] * npp), *([cv] * npp))


def _post_kernel(x_ref, y_ref, wp_ref, g_ref, wi_ref, wo_ref, gf_ref, o_ref, *, final):
    x = x_ref[...] + jnp.dot(y_ref[...].astype(BF16), wp_ref[...], preferred_element_type=F32)
    n = _rms(x, g_ref[...], NORM_EPS).astype(BF16)
    out = x
    for c in range(N_FF_CHUNKS):
        lo, hi = c * FF_CHUNK, (c + 1) * FF_CHUNK
        gc = jnp.dot(n, wi_ref[:, lo:hi], preferred_element_type=F32)
        uc = jnp.dot(n, wi_ref[:, D_FF + lo:D_FF + hi], preferred_element_type=F32)
        act = (gc * jax.nn.sigmoid(gc) * uc).astype(BF16)
        out = out + jnp.dot(act, wo_ref[lo:hi, :], preferred_element_type=F32)
    if final:
        out = _rms(out, gf_ref[...], NORM_EPS)
    o_ref[...] = out


def _post(x, y, wp, g, wi, wo, gf, *, tm, final):
    m = x.shape[0]
    row_spec = pl.BlockSpec((tm, D_MODEL), lambda i: (i, 0))
    return pl.pallas_call(
        functools.partial(_post_kernel, final=final),
        grid=(m // tm,),
        in_specs=[row_spec, row_spec, _const_spec((D_MODEL, D_MODEL)), _const_spec((1, D_MODEL)),
                  _const_spec((D_MODEL, 2 * D_FF)), _const_spec((D_FF, D_MODEL)), _const_spec((1, D_MODEL))],
        out_specs=row_spec,
        out_shape=jax.ShapeDtypeStruct((m, D_MODEL), F32),
        compiler_params=_cparams(("arbitrary",)),
        name="post",
    )(x, y, wp, g, wi, wo, gf)


def kernel(x_prompt, x_sample, cache_k, cache_v, page_table, state_conv, state_h, norm_mix, w_rec_in, conv_w,
           conv_b, gate_x_w, gate_x_b, gate_a_w, gate_a_b, lru_lambda, w_rec_out, w_qkv, lambda_q1, lambda_k1,
           lambda_q2, lambda_k2, subln_g, w_attn_out, norm_ffn, w_ffn_in, w_ffn_out, norm_final):
    B, T, _ = x_prompt.shape
    DB, Tn, _ = x_sample.shape
    past = page_table.shape[1] * PAGE_SIZE
    row = lambda a: a.reshape(1, -1)

    hp = x_prompt.reshape(B * T, D_MODEL)
    hs = x_sample.reshape(DB * Tn, D_MODEL)
    tabs_p = _rope_tables(jnp.arange(T, dtype=F32))
    tabs_s = _rope_tables(jnp.tile(past + jnp.arange(Tn, dtype=F32), DB))

    kt_all = v_all = None
    pconv, ph = [], []
    sk, sv, sconv, sh = [], [], [], []
    for i in range(DEPTH):
        g_mix = row(norm_mix[i])
        if i % N_MIXERS == 0:
            r = i // N_MIXERS
            w = (w_rec_in[r].astype(BF16), conv_w[r], row(conv_b[r]),
                 gate_x_w[r].astype(BF16), row(gate_x_b[r]), gate_a_w[r].astype(BF16), row(gate_a_b[r]),
                 row(lru_lambda[r]))
            yp, c_p, h_p = _rec_prompt(hp, g_mix, *w, batch=B, seq=T)
            to_tm = lambda t, n: t.reshape(DB, n, D_RNN).swapaxes(0, 1).reshape(n * DB, D_RNN)
            from_tm = lambda t, n: t.reshape(n, DB, D_RNN).swapaxes(0, 1)
            ys, c_s, h_s = _rec_sample(to_tm(hs, Tn), to_tm(state_conv[r], CONV_W - 1), state_h[r], g_mix, *w,
                                       nb=DB, nt=Tn)
            ys = from_tm(ys, Tn).reshape(DB * Tn, D_RNN)
            pconv.append(c_p); ph.append(h_p.reshape(B, D_RNN))
            sconv.append(from_tm(c_s, CONV_W - 1)); sh.append(h_s)
            w_proj = w_rec_out[r].astype(BF16)
        else:
            a = i // N_MIXERS
            lam_init = 0.8 - 0.6 * math.exp(-0.3 * i)
            lams = (row(lambda_q1[a]), row(lambda_k1[a]), row(lambda_q2[a]), row(lambda_k2[a]))
            wq = w_qkv[a].astype(BF16)
            q_p, kt_all, v_all, ktb, vb = _qkv_prompt(hp, g_mix, wq, *tabs_p, kt_all, v_all,
                                                      layer=a, batch=B, seq=T)
            yp = _attn_prompt(*lams, row(subln_g[a]), q_p, ktb, vb, batch=B, seq=T, lam_init=lam_init)
            q_s, k_s, v_s = _qkv_sample(hs, g_mix, wq, *tabs_s)
            seq3 = lambda t: t.reshape(DB, Tn, D_MODEL)
            ys = _attn_sample(page_table, *lams, row(subln_g[a]), seq3(q_s), seq3(k_s),
                              v_s.reshape(DB, Tn, N_HEADS, V_DIM), cache_k, cache_v,
                              layer=a, lam_init=lam_init).reshape(DB * Tn, D_MODEL)
            sk.append(k_s.reshape(DB, Tn, N_HEADS, 2, HEAD_DIM)); sv.append(v_s.reshape(DB, Tn, N_HEADS, V_DIM))
            w_proj = w_attn_out[a].astype(BF16)

        final = i == DEPTH - 1
        post_w = (w_proj, row(norm_ffn[i]), w_ffn_in[i].astype(BF16), w_ffn_out[i].astype(BF16), row(norm_final))
        hp = _post(hp, yp, *post_w, tm=ROW_TILE, final=final)
        hs = _post(hs, ys, *post_w, tm=DB * Tn, final=final)

    prompt_k = jnp.transpose(kt_all.reshape(N_ATTN_LAYERS, B, N_HEADS, 2, HEAD_DIM, T), (0, 1, 5, 2, 3, 4))
    return (hp.reshape(B, T, D_MODEL), hs.reshape(DB, Tn, D_MODEL),
            prompt_k, v_all, jnp.stack(pconv), jnp.stack(ph),
            jnp.stack(sk), jnp.stack(sv), jnp.stack(sconv), jnp.stack(sh))
```

```python
import functools
import math

import jax
import jax.numpy as jnp
from jax import lax
from jax.experimental import pallas as pl
from jax.experimental.pallas import tpu as pltpu

F32 = jnp.float32
BF16 = jnp.bfloat16

D_MODEL = 1024
DEPTH = 4
N_MIXERS = 2
N_ATTN_LAYERS = DEPTH // N_MIXERS
D_RNN = D_MODEL
N_RNN_BLOCKS = 4
RNN_BLOCK = D_RNN // N_RNN_BLOCKS
CONV_W = 4
LRU_C = 8.0
HEAD_DIM = 64
N_HEADS = D_MODEL // (2 * HEAD_DIM)
V_DIM = 2 * HEAD_DIM
ROT_DIM = HEAD_DIM // 4
ROPE_THETA = 500000.0
D_FF = -(-8 * D_MODEL // (3 * 256)) * 256
PAGE_SIZE = 128
NORM_EPS = 1e-6
SUBLN_EPS = 1e-5
NEG_INF = -1e30
QK_SCALE = HEAD_DIM ** -0.5 * math.log2(math.e)

V7X_VMEM_BYTES = 64 * 1024 * 1024
V7X_SUBLANES = 8
V7X_LANES = 128
V7X_MXU_DIM = 256

ROW_TILE = 512
FF_CHUNK = V7X_MXU_DIM
N_FF_CHUNKS = D_FF // FF_CHUNK
ATTN_TILE = 512
PAGES_PER_STEP = 8
VMEM_LIMIT = V7X_VMEM_BYTES - 8 * 1024 * 1024


def _cparams(semantics):
    return pltpu.CompilerParams(dimension_semantics=semantics, vmem_limit_bytes=VMEM_LIMIT)


def _const_spec(shape):
    nd = len(shape)
    return pl.BlockSpec(shape, lambda *_: (0,) * nd, pipeline_mode=pl.Buffered(1))


def _rms(x, g, eps):
    ms = jnp.mean(x * x, axis=-1, keepdims=True)
    return x * lax.rsqrt(ms + eps) * g


def _log_sigmoid(x):
    return jnp.minimum(x, 0.0) - jnp.log1p(jnp.exp(-jnp.abs(x)))


def _lru_coeffs(xc, wx_ref, bx, wa_ref, ba, lam):
    xcb = xc.astype(BF16)

    def gate(w_ref, b):
        z = [jnp.dot(xcb[:, n * RNN_BLOCK:(n + 1) * RNN_BLOCK], w_ref[n], preferred_element_type=F32)
             for n in range(N_RNN_BLOCKS)]
        return jax.nn.sigmoid(jnp.concatenate(z, axis=-1) + b)

    gx = gate(wx_ref, bx)
    ga = gate(wa_ref, ba)
    log_a = ga * (LRU_C * _log_sigmoid(lam))
    a = jnp.exp(log_a)
    mult = jnp.sqrt(jnp.maximum(-jnp.tanh(log_a) * (a * a + 1.0), 0.0))
    return a, mult * gx * xc


def _rec_prompt_kernel(x_ref, g_ref, win_ref, cw_ref, cb_ref, wx_ref, bx_ref, wa_ref, ba_ref, lam_ref,
                       y_ref, conv_ref, h_ref,
                       xe_ref, a_ref, u_ref, hs_ref, hc_ref, *, tm):
    ti = pl.program_id(1)
    pad = V7X_SUBLANES

    @pl.when(ti == 0)
    def _():
        xe_ref[...] = jnp.zeros_like(xe_ref)
        hc_ref[...] = jnp.zeros_like(hc_ref)

    n = _rms(x_ref[...], g_ref[...], NORM_EPS).astype(BF16)
    y = jnp.dot(n, win_ref[...], preferred_element_type=F32)
    xr = y[:, :D_RNN]
    gate = y[:, D_RNN:]

    cw = cw_ref[...]
    prev = xe_ref[...]
    row = lax.broadcasted_iota(jnp.int32, (pad, D_RNN), 0)
    xc = cb_ref[...]
    for k in range(CONV_W):
        shift = CONV_W - 1 - k
        if shift == 0:
            xs = xr
        else:
            rolled = pltpu.roll(xr, shift, 0)
            head = jnp.where(row < shift, pltpu.roll(prev, shift, 0), rolled[0:pad, :])
            xs = jnp.concatenate([head, rolled[pad:, :]], axis=0)
        xc = xc + xs * cw[k:k + 1, :]
    xe_ref[...] = xr[tm - pad:tm, :]

    a, u = _lru_coeffs(xc, wx_ref, bx_ref[...], wa_ref, ba_ref[...], lam_ref[...])
    a_ref[...] = a
    u_ref[...] = u

    def group(gidx, h):
        r0 = pl.multiple_of(gidx * pad, pad)
        ag = a_ref[pl.ds(r0, pad), :]
        ug = u_ref[pl.ds(r0, pad), :]
        for k in (1, 2, 4):
            m = row >= k
            a_s = jnp.where(m, pltpu.roll(ag, k, 0), 1.0)
            u_s = jnp.where(m, pltpu.roll(ug, k, 0), 0.0)
            ug = ug + ag * u_s
            ag = ag * a_s
        hrows = ag * h + ug
        hs_ref[pl.ds(r0, pad), :] = hrows
        return hrows[pad - 1:pad, :]

    h_last = lax.fori_loop(0, tm // pad, group, hc_ref[...])
    hc_ref[...] = h_last

    y_ref[...] = (hs_ref[...] * jax.nn.gelu(gate)).astype(y_ref.dtype)

    @pl.when(ti == pl.num_programs(1) - 1)
    def _():
        conv_ref[0] = xe_ref[pad - (CONV_W - 1):pad, :]
        h_ref[0] = h_last


def _rec_prompt(x, g, win, cw, cb, wx, bx, wa, ba, lam, *, batch, seq):
    tm = ROW_TILE
    nt = seq // tm
    row_spec = pl.BlockSpec((tm, D_MODEL), lambda b, t: (b * nt + t, 0))
    return pl.pallas_call(
        functools.partial(_rec_prompt_kernel, tm=tm),
        grid=(batch, nt),
        in_specs=[row_spec, _const_spec((1, D_MODEL)), _const_spec((D_MODEL, 2 * D_RNN)),
                  _const_spec((CONV_W, D_RNN)), _const_spec((1, D_RNN)),
                  _const_spec((N_RNN_BLOCKS, RNN_BLOCK, RNN_BLOCK)), _const_spec((1, D_RNN)),
                  _const_spec((N_RNN_BLOCKS, RNN_BLOCK, RNN_BLOCK)), _const_spec((1, D_RNN)),
                  _const_spec((1, D_RNN))],
        out_specs=[row_spec,
                   pl.BlockSpec((1, CONV_W - 1, D_RNN), lambda b, t: (b, 0, 0)),
                   pl.BlockSpec((1, 1, D_RNN), lambda b, t: (b, 0, 0))],
        out_shape=[jax.ShapeDtypeStruct((batch * seq, D_RNN), BF16),
                   jax.ShapeDtypeStruct((batch, CONV_W - 1, D_RNN), F32),
                   jax.ShapeDtypeStruct((batch, 1, D_RNN), F32)],
        scratch_shapes=[pltpu.VMEM((V7X_SUBLANES, D_RNN), F32),
                        pltpu.VMEM((tm, D_RNN), F32), pltpu.VMEM((tm, D_RNN), F32),
                        pltpu.VMEM((tm, D_RNN), F32), pltpu.VMEM((1, D_RNN), F32)],
        compiler_params=_cparams(("arbitrary", "arbitrary")),
        name="rec_prompt",
    )(x, g, win, cw, cb, wx, bx, wa, ba, lam)


def _rec_sample_kernel(x_ref, sc_ref, h0_ref, g_ref, win_ref, cw_ref, cb_ref, wx_ref, bx_ref, wa_ref, ba_ref,
                       lam_ref, y_ref, conv_ref, h_ref, *, nb, nt):
    n = _rms(x_ref[...], g_ref[...], NORM_EPS).astype(BF16)
    y = jnp.dot(n, win_ref[...], preferred_element_type=F32)
    gate = y[:, D_RNN:]

    nbuf = CONV_W - 1
    xe = [sc_ref[k * nb:(k + 1) * nb, :] for k in range(nbuf)]
    xe += [y[t * nb:(t + 1) * nb, :D_RNN] for t in range(nt)]
    cw = cw_ref[...]
    slabs = []
    for t in range(nt):
        xc = cb_ref[...]
        for k in range(CONV_W):
            xc = xc + xe[t + k] * cw[k:k + 1, :]
        slabs.append(xc)
    xc = jnp.concatenate(slabs, axis=0)

    a, u = _lru_coeffs(xc, wx_ref, bx_ref[...], wa_ref, ba_ref[...], lam_ref[...])
    h = h0_ref[...]
    for t in range(nt):
        rows = slice(t * nb, (t + 1) * nb)
        h = a[rows] * h + u[rows]
        y_ref[rows, :] = h * jax.nn.gelu(gate[rows])
    h_ref[...] = h
    for k in range(nbuf):
        conv_ref[k * nb:(k + 1) * nb, :] = xe[len(xe) - nbuf + k]


def _rec_sample(x, sc, h0, g, win, cw, cb, wx, bx, wa, ba, lam, *, nb, nt):
    m = nb * nt
    nbuf = CONV_W - 1
    return pl.pallas_call(
        functools.partial(_rec_sample_kernel, nb=nb, nt=nt),
        out_shape=[jax.ShapeDtypeStruct((m, D_RNN), F32),
                   jax.ShapeDtypeStruct((nb * nbuf, D_RNN), F32),
                   jax.ShapeDtypeStruct((nb, D_RNN), F32)],
        compiler_params=pltpu.CompilerParams(vmem_limit_bytes=VMEM_LIMIT),
        name="rec_sample",
    )(x, sc, h0, g, win, cw, cb, wx, bx, wa, ba, lam)


def _rope(xh, c, sneg, spos):
    half = ROT_DIM // 2
    return xh * c + pltpu.roll(xh, V7X_LANES - half, 1) * sneg + pltpu.roll(xh, half, 1) * spos


def _rope_tables(pos):
    half = ROT_DIM // 2
    inv = jnp.power(ROPE_THETA, -jnp.arange(half, dtype=F32) * 2.0 / ROT_DIM)
    ang = pos[:, None] * inv[None, :]
    cos, sin = jnp.cos(ang), jnp.sin(ang)
    n = pos.shape[0]
    ones = jnp.ones((n, HEAD_DIM - ROT_DIM), F32)
    zeros = jnp.zeros((n, HEAD_DIM - ROT_DIM), F32)
    zh = jnp.zeros((n, half), F32)
    c = jnp.concatenate([cos, cos, ones], axis=-1)
    sneg = jnp.concatenate([-sin, zh, zeros], axis=-1)
    spos = jnp.concatenate([zh, sin, zeros], axis=-1)
    rep = V7X_LANES // HEAD_DIM
    return tuple(jnp.tile(t, (1, rep)) for t in (c, sneg, spos))


def _qkv_sample_kernel(x_ref, g_ref, w_ref, cos_ref, sneg_ref, spos_ref, q_ref, k_ref, v_ref):
    n = _rms(x_ref[...], g_ref[...], NORM_EPS).astype(BF16)
    qkv = jnp.dot(n, w_ref[...], preferred_element_type=F32)
    tabs = (cos_ref[...], sneg_ref[...], spos_ref[...])
    for h in range(N_HEADS):
        lo, hi = h * V_DIM, (h + 1) * V_DIM
        q_ref[:, lo:hi] = _rope(qkv[:, lo:hi], *tabs) * QK_SCALE
        k_ref[:, lo:hi] = _rope(qkv[:, D_MODEL + lo:D_MODEL + hi], *tabs)
    v_ref[...] = qkv[:, 2 * D_MODEL:]


def _qkv_sample(x, g, w, cos, sneg, spos):
    out = jax.ShapeDtypeStruct((x.shape[0], D_MODEL), F32)
    return pl.pallas_call(
        _qkv_sample_kernel,
        out_shape=[out, out, out],
        compiler_params=pltpu.CompilerParams(vmem_limit_bytes=VMEM_LIMIT),
        name="qkv_sample",
    )(x, g, w, cos, sneg, spos)


def _qkv_prompt_kernel(x_ref, g_ref, w_ref, cos_ref, sneg_ref, spos_ref, *rest):
    q_ref, kt_ref, vo_ref, ktb_ref, vb_ref = rest[-5:]
    n = _rms(x_ref[...], g_ref[...], NORM_EPS).astype(BF16)
    qkv = jnp.dot(n, w_ref[...], preferred_element_type=F32)
    tabs = (cos_ref[...], sneg_ref[...], spos_ref[...])
    for h in range(N_HEADS):
        lo, hi = h * V_DIM, (h + 1) * V_DIM
        q_ref[h] = (_rope(qkv[:, lo:hi], *tabs) * QK_SCALE).astype(q_ref.dtype)
        kt = _rope(qkv[:, D_MODEL + lo:D_MODEL + hi], *tabs).T
        kt_ref[lo:hi, :] = kt
        ktb_ref[lo:hi, :] = kt.astype(BF16)
        vh = qkv[:, 2 * D_MODEL + lo:2 * D_MODEL + hi]
        vo_ref[:, h, :] = vh
        vb_ref[h] = vh.astype(BF16)


def _qkv_prompt(x, g, w, cos, sneg, spos, kt_all, v_all, *, layer, batch, seq):
    tm = ROW_TILE
    nt = seq // tm
    row_spec = pl.BlockSpec((tm, D_MODEL), lambda b, t: (b * nt + t, 0))
    tab_spec = pl.BlockSpec((tm, V7X_LANES), lambda b, t: (t, 0))
    in_specs = [row_spec, _const_spec((1, D_MODEL)), _const_spec((D_MODEL, 3 * D_MODEL)),
                tab_spec, tab_spec, tab_spec]
    args = [x, g, w, cos, sneg, spos]
    aliases = {}
    if kt_all is not None:
        in_specs += [pl.BlockSpec(memory_space=pl.ANY)] * 2
        aliases = {len(args): 1, len(args) + 1: 2}
        args += [kt_all, v_all]
    return pl.pallas_call(
        _qkv_prompt_kernel,
        grid=(batch, nt),
        in_specs=in_specs,
        out_specs=[pl.BlockSpec((None, N_HEADS, tm, V_DIM), lambda b, t: (b, 0, t, 0)),
                   pl.BlockSpec((None, None, D_MODEL, tm), lambda b, t: (layer, b, 0, t)),
                   pl.BlockSpec((None, None, tm, N_HEADS, V_DIM), lambda b, t: (layer, b, t, 0, 0)),
                   pl.BlockSpec((None, D_MODEL, tm), lambda b, t: (b, 0, t)),
                   pl.BlockSpec((None, N_HEADS, tm, V_DIM), lambda b, t: (b, 0, t, 0))],
        out_shape=[jax.ShapeDtypeStruct((batch, N_HEADS, seq, V_DIM), BF16),
                   jax.ShapeDtypeStruct((N_ATTN_LAYERS, batch, D_MODEL, seq), F32),
                   jax.ShapeDtypeStruct((N_ATTN_LAYERS, batch, seq, N_HEADS, V_DIM), F32),
                   jax.ShapeDtypeStruct((batch, D_MODEL, seq), BF16),
                   jax.ShapeDtypeStruct((batch, N_HEADS, seq, V_DIM), BF16)],
        input_output_aliases=aliases,
        compiler_params=_cparams(("arbitrary", "arbitrary")),
        name="qkv_prompt",
    )(*args)


def _diff_lambda(lq1, lk1, lq2, lk2, lam_init):
    s1 = jnp.sum(lq1 * lk1, axis=-1, keepdims=True)
    s2 = jnp.sum(lq2 * lk2, axis=-1, keepdims=True)
    return jnp.exp(s1) - jnp.exp(s2) + lam_init


HEADS_PER_UNIT = N_HEADS // 2


def _prompt_attn_unit(qi, ki, half, lq1_ref, lk1_ref, lq2_ref, lk2_ref, sg_ref, q_ref, kt_ref, v_ref, o_ref,
                      q2_ref, m_ref, acc_ref, oh_ref, *, tile, lam_init):
    h0 = half * HEADS_PER_UNIT

    @pl.when((ki == 0) & (half == 0))
    def _():
        q = q_ref[...]
        lane = lax.broadcasted_iota(jnp.int32, q.shape, 2)
        zero = jnp.zeros_like(q)
        q2_ref[:, 0:tile, :] = jnp.where(lane < HEAD_DIM, q, zero)
        q2_ref[:, tile:2 * tile, :] = jnp.where(lane >= HEAD_DIM, q, zero)
        m_ref[...] = jnp.full_like(m_ref, NEG_INF)
        acc_ref[...] = jnp.zeros_like(acc_ref)

    def update(h, diagonal):
        kt = kt_ref[pl.ds(pl.multiple_of(h * V_DIM, V_DIM), V_DIM), :]
        s = jnp.dot(q2_ref[h], kt, preferred_element_type=F32)
        if diagonal:
            row = lax.broadcasted_iota(jnp.int32, s.shape, 0)
            col = lax.broadcasted_iota(jnp.int32, s.shape, 1)
            s = jnp.where(col <= jnp.where(row >= tile, row - tile, row), s, NEG_INF)
        m_prev = m_ref[h]
        m_new = jnp.maximum(m_prev, jnp.max(s, axis=-1, keepdims=True))
        alpha = jnp.exp2(m_prev - m_new)
        p = jnp.exp2(s - jnp.tile(m_new, (1, tile // V7X_LANES))).astype(BF16)
        v1 = jnp.concatenate([v_ref[h], jnp.ones((tile, V7X_LANES), BF16)], axis=1)
        acc_ref[h] = jnp.tile(alpha, (1, 2)) * acc_ref[h] + jnp.dot(p, v1, preferred_element_type=F32)
        m_ref[h] = m_new

    @pl.when(ki < qi)
    def _():
        def head(i, carry):
            update(h0 + i, False)
            return carry
        lax.fori_loop(0, HEADS_PER_UNIT, head, 0)

    @pl.when(ki == qi)
    def _():
        lam = _diff_lambda(lq1_ref[...], lk1_ref[...], lq2_ref[...], lk2_ref[...], lam_init)

        def head(i, carry):
            h = h0 + i
            update(h, True)
            acc = acc_ref[h]
            o = acc[:, 0:V_DIM] / acc[:, V_DIM:2 * V_DIM]
            o = o[0:tile, :] - lam * o[tile:2 * tile, :]
            oh_ref[h] = (_rms(o, sg_ref[...], SUBLN_EPS) * (1.0 - lam_init)).astype(oh_ref.dtype)
            return carry
        lax.fori_loop(0, HEADS_PER_UNIT, head, 0)
        for hf in range(N_HEADS // HEADS_PER_UNIT):
            @pl.when(half == hf)
            def _():
                for h in range(hf * HEADS_PER_UNIT, (hf + 1) * HEADS_PER_UNIT):
                    o_ref[:, h * V_DIM:(h + 1) * V_DIM] = oh_ref[h]


def _sample_attn_step(j, last, lq1_ref, lk1_ref, lq2_ref, lk2_ref, sg_ref, e_ref, q_ref, kn_ref, vn_ref,
                      k_refs, v_refs, o_ref, qt_ref, qtb_ref, m_ref, l_ref, acc_ref, *, nt, lam_init):
    npp = len(k_refs)
    nrow = nt * N_HEADS

    @pl.when(j == 0)
    def _():
        q = q_ref[0]
        sub = lax.broadcasted_iota(jnp.int32, (N_HEADS, D_MODEL), 0)
        col = lax.broadcasted_iota(jnp.int32, (N_HEADS, D_MODEL), 1)
        for c in range(2):
            keep = (col // V_DIM == sub) & ((col // HEAD_DIM) % 2 == c)
            for t in range(nt):
                r0 = c * nrow + t * N_HEADS
                qt_ref[r0:r0 + N_HEADS, :] = jnp.where(keep, jnp.broadcast_to(q[t:t + 1, :], (N_HEADS, D_MODEL)), 0.0)
        qtb_ref[...] = qt_ref[...].astype(BF16)

        qt = qt_ref[...]
        kn = kn_ref[0]
        r = lax.broadcasted_iota(jnp.int32, (2 * nrow, 1), 0)
        t_row = (r // N_HEADS) % nt
        sn = []
        for t in range(nt):
            st = jnp.sum(qt * kn[t:t + 1, :], axis=-1, keepdims=True)
            sn.append(jnp.where(t <= t_row, st, NEG_INF))
        m = sn[0]
        for st in sn[1:]:
            m = jnp.maximum(m, st)
        l = jnp.zeros_like(m)
        acc = jnp.zeros(acc_ref.shape, F32)
        for t in range(nt):
            pt = jnp.exp2(sn[t] - m)
            l = l + pt
            vt = jnp.concatenate([vn_ref[0, t]] * (2 * nt), axis=0)
            acc = acc + pt * vt
        m_ref[...] = m
        l_ref[...] = l
        acc_ref[...] = acc

    qtb = qtb_ref[...]
    s = jnp.concatenate([jnp.dot(qtb, k_refs[i][...].astype(BF16), preferred_element_type=F32)
                         for i in range(npp)], axis=1)
    m_prev = m_ref[...]
    m_new = jnp.maximum(m_prev, jnp.max(s, axis=-1, keepdims=True))
    alpha = jnp.exp2(m_prev - m_new)
    p = jnp.exp2(s - m_new)
    l_ref[...] = alpha * l_ref[...] + jnp.sum(p, axis=-1, keepdims=True)
    m_ref[...] = m_new
    pb = p.astype(BF16)
    pstack = jnp.concatenate([pb[:, i * PAGE_SIZE:(i + 1) * PAGE_SIZE] for i in range(npp)], axis=0)
    pe = jnp.dot(pstack, e_ref[...], preferred_element_type=F32)
    hh = lax.broadcasted_iota(jnp.int32, (N_HEADS, pe.shape[1]), 0)
    cc = lax.broadcasted_iota(jnp.int32, (N_HEADS, pe.shape[1]), 1)
    own = (hh == cc % N_HEADS)[None]
    pe = jnp.where(own, pe.reshape(pe.shape[0] // N_HEADS, N_HEADS, pe.shape[1]), 0.0).reshape(pe.shape)
    pe = pe.astype(BF16)
    acc = alpha * acc_ref[...]
    for i in range(npp):
        vb = v_refs[i][...].astype(BF16)
        acc = acc + jnp.dot(pe[i * 2 * nrow:(i + 1) * 2 * nrow, :], vb, preferred_element_type=F32)
    acc_ref[...] = acc

    @pl.when(last)
    def _():
        acc = acc_ref[...] / l_ref[...]
        lam = _diff_lambda(lq1_ref[...], lk1_ref[...], lq2_ref[...], lk2_ref[...], lam_init)
        o = acc[0:nrow, :] - lam * acc[nrow:2 * nrow, :]
        o = _rms(o, sg_ref[...], SUBLN_EPS) * (1.0 - lam_init)
        for t in range(nt):
            o_ref[0, t] = o[t * N_HEADS:(t + 1) * N_HEADS, :]


N_SAMPLE_IN = 9
N_SAMPLE_SCRATCH = 5


def _attn_kernel(pt_ref, act_ref, ub_ref, uqi_ref, uki_ref, uhalf_ref, *refs, nt, tile, lam_init):
    del pt_ref, ub_ref
    npp = PAGES_PER_STEP
    sample_in = refs[:N_SAMPLE_IN]
    k_refs = refs[N_SAMPLE_IN:N_SAMPLE_IN + npp]
    v_refs = refs[N_SAMPLE_IN + npp:N_SAMPLE_IN + 2 * npp]
    rest = refs[N_SAMPLE_IN + 2 * npp:]
    q_ref, kt_ref, vp_ref, os_ref, op_ref = rest[:5]
    sample_scratch = rest[5:5 + N_SAMPLE_SCRATCH]
    prompt_scratch = rest[5 + N_SAMPLE_SCRATCH:]
    j = pl.program_id(1)
    g = pl.program_id(0) * pl.num_programs(1) + j

    _sample_attn_step(j, j == pl.num_programs(1) - 1, *sample_in, k_refs, v_refs, os_ref, *sample_scratch,
                      nt=nt, lam_init=lam_init)

    @pl.when(act_ref[g] == 1)
    def _():
        lams_sg = sample_in[:5]
        _prompt_attn_unit(uqi_ref[g], uki_ref[g], uhalf_ref[g], *lams_sg, q_ref, kt_ref, vp_ref, op_ref,
                          *prompt_scratch, tile=tile, lam_init=lam_init)


def _attention(page_table, lq1, lk1, lq2, lk2, sg, q_s, kn, vn, cache_k, cache_v, q_p, ktb, vb, *,
               layer, lam_init, batch, seq):
    nb, nt, _ = q_s.shape
    n_layers, n_pool = cache_k.shape[:2]
    n_pages = page_table.shape[1]
    npp = PAGES_PER_STEP
    nj = n_pages // npp
    nrow2 = 2 * nt * N_HEADS
    vrows = PAGE_SIZE * N_HEADS
    tile = ATTN_TILE
    nq = seq // tile

    n_steps = nb * nj
    units = [(b, i, k, hf) for b in range(batch) for i in range(nq) for k in range(i + 1)
             for hf in range(N_HEADS // HEADS_PER_UNIT)]
    assert len(units) <= n_steps
    unit_of_step = [s * len(units) // n_steps for s in range(n_steps)]
    active = [1] + [int(unit_of_step[s] != unit_of_step[s - 1]) for s in range(1, n_steps)]
    tab = lambda col: jnp.array([units[u][col] for u in unit_of_step], jnp.int32)
    tables = (jnp.array(active, jnp.int32), tab(0), tab(1), tab(2), tab(3))

    step = lambda b, j: b * nj + j

    def page_map(i):
        return lambda b, j, pt, *_: (layer, pt[b, j * npp + i], 0, 0)

    const2 = lambda shape: pl.BlockSpec(shape, lambda b, j, *_: (0, 0))
    seq_spec = pl.BlockSpec((1, nt, D_MODEL), lambda b, j, *_: (b, 0, 0))
    head_spec = pl.BlockSpec((1, nt, N_HEADS, V_DIM), lambda b, j, *_: (b, 0, 0, 0))
    lam_spec = const2((1, HEAD_DIM))
    q_spec = pl.BlockSpec((None, N_HEADS, tile, V_DIM),
                          lambda b, j, pt, act, ub, uqi, uki, uh: (ub[step(b, j)], 0, uqi[step(b, j)], 0))
    kt_spec = pl.BlockSpec((None, D_MODEL, tile),
                           lambda b, j, pt, act, ub, uqi, uki, uh: (ub[step(b, j)], 0, uki[step(b, j)]))
    v_spec = pl.BlockSpec((None, N_HEADS, tile, V_DIM),
                          lambda b, j, pt, act, ub, uqi, uki, uh: (ub[step(b, j)], 0, uki[step(b, j)], 0))
    op_spec = pl.BlockSpec((tile, D_MODEL),
                           lambda b, j, pt, act, ub, uqi, uki, uh: (ub[step(b, j)] * nq + uqi[step(b, j)], 0))
    grid_spec = pltpu.PrefetchScalarGridSpec(
        num_scalar_prefetch=1 + len(tables),
        grid=(nb, nj),
        in_specs=[lam_spec, lam_spec, lam_spec, lam_spec, const2((1, V_DIM)), const2((PAGE_SIZE, vrows)),
                  seq_spec, seq_spec, head_spec]
                 + [pl.BlockSpec((None, None, D_MODEL, PAGE_SIZE), page_map(i)) for i in range(npp)]
                 + [pl.BlockSpec((None, None, vrows, V_DIM), page_map(i)) for i in range(npp)]
                 + [q_spec, kt_spec, v_spec],
        out_specs=[head_spec, op_spec],
        scratch_shapes=[pltpu.VMEM((nrow2, D_MODEL), F32), pltpu.VMEM((nrow2, D_MODEL), BF16),
                        pltpu.VMEM((nrow2, 1), F32), pltpu.VMEM((nrow2, 1), F32),
                        pltpu.VMEM((nrow2, V_DIM), F32),
                        pltpu.VMEM((N_HEADS, 2 * tile, V_DIM), BF16),
                        pltpu.VMEM((N_HEADS, 2 * tile, V7X_LANES), F32),
                        pltpu.VMEM((N_HEADS, 2 * tile, 2 * V_DIM), F32),
                        pltpu.VMEM((N_HEADS, tile, V_DIM), BF16)],
    )
    ck = jnp.transpose(cache_k, (0, 1, 3, 4, 5, 2)).reshape(n_layers, n_pool, D_MODEL, PAGE_SIZE)
    cv = cache_v.reshape(n_layers, n_pool, vrows, V_DIM)
    spread = (jnp.arange(vrows)[None, :] // N_HEADS == jnp.arange(PAGE_SIZE)[:, None]).astype(BF16)
    return pl.pallas_call(
        functools.partial(_attn_kernel, nt=nt, tile=tile, lam_init=lam_init),
        grid_spec=grid_spec,
        out_shape=[jax.ShapeDtypeStruct((nb, nt, N_HEADS, V_DIM), F32),
                   jax.ShapeDtypeStruct((batch * seq, D_MODEL), BF16)],
        compiler_params=_cparams(("arbitrary", "arbitrary")),
        name="attention",
    )(page_table, *tables, lq1, lk1, lq2, lk2, sg, spread, q_s, kn, vn, *([ck] * npp), *([cv] * npp),
      q_p, ktb, vb)


def _post_kernel(x_ref, y_ref, wp_ref, g_ref, wi_ref, wo_ref, gf_ref, o_ref, *, final):
    x = x_ref[...] + jnp.dot(y_ref[...].astype(BF16), wp_ref[...], preferred_element_type=F32)
    n = _rms(x, g_ref[...], NORM_EPS).astype(BF16)
    out = x
    for c in range(N_FF_CHUNKS):
        lo, hi = c * FF_CHUNK, (c + 1) * FF_CHUNK
        gc = jnp.dot(n, wi_ref[:, lo:hi], preferred_element_type=F32)
        uc = jnp.dot(n, wi_ref[:, D_FF + lo:D_FF + hi], preferred_element_type=F32)
        act = (gc * jax.nn.sigmoid(gc) * uc).astype(BF16)
        out = out + jnp.dot(act, wo_ref[lo:hi, :], preferred_element_type=F32)
    if final:
        out = _rms(out, gf_ref[...], NORM_EPS)
    o_ref[...] = out


def _post(x, y, wp, g, wi, wo, gf, *, tm, final):
    m = x.shape[0]
    row_spec = pl.BlockSpec((tm, D_MODEL), lambda i: (i, 0))
    return pl.pallas_call(
        functools.partial(_post_kernel, final=final),
        grid=(m // tm,),
        in_specs=[row_spec, row_spec, _const_spec((D_MODEL, D_MODEL)), _const_spec((1, D_MODEL)),
                  _const_spec((D_MODEL, 2 * D_FF)), _const_spec((D_FF, D_MODEL)), _const_spec((1, D_MODEL))],
        out_specs=row_spec,
        out_shape=jax.ShapeDtypeStruct((m, D_MODEL), F32),
        compiler_params=_cparams(("arbitrary",)),
        name="post",
    )(x, y, wp, g, wi, wo, gf)


def kernel(x_prompt, x_sample, cache_k, cache_v, page_table, state_conv, state_h, norm_mix, w_rec_in, conv_w,
           conv_b, gate_x_w, gate_x_b, gate_a_w, gate_a_b, lru_lambda, w_rec_out, w_qkv, lambda_q1, lambda_k1,
           lambda_q2, lambda_k2, subln_g, w_attn_out, norm_ffn, w_ffn_in, w_ffn_out, norm_final):
    B, T, _ = x_prompt.shape
    DB, Tn, _ = x_sample.shape
    past = page_table.shape[1] * PAGE_SIZE
    row = lambda a: a.reshape(1, -1)

    hp = x_prompt.reshape(B * T, D_MODEL)
    hs = x_sample.reshape(DB * Tn, D_MODEL)
    tabs_p = _rope_tables(jnp.arange(T, dtype=F32))
    tabs_s = _rope_tables(jnp.tile(past + jnp.arange(Tn, dtype=F32), DB))

    kt_all = v_all = None
    pconv, ph = [], []
    sk, sv, sconv, sh = [], [], [], []
    for i in range(DEPTH):
        g_mix = row(norm_mix[i])
        if i % N_MIXERS == 0:
            r = i // N_MIXERS
            w = (w_rec_in[r].astype(BF16), conv_w[r], row(conv_b[r]),
                 gate_x_w[r].astype(BF16), row(gate_x_b[r]), gate_a_w[r].astype(BF16), row(gate_a_b[r]),
                 row(lru_lambda[r]))
            yp, c_p, h_p = _rec_prompt(hp, g_mix, *w, batch=B, seq=T)
            to_tm = lambda t, n: t.reshape(DB, n, D_RNN).swapaxes(0, 1).reshape(n * DB, D_RNN)
            from_tm = lambda t, n: t.reshape(n, DB, D_RNN).swapaxes(0, 1)
            ys, c_s, h_s = _rec_sample(to_tm(hs, Tn), to_tm(state_conv[r], CONV_W - 1), state_h[r], g_mix, *w,
                                       nb=DB, nt=Tn)
            ys = from_tm(ys, Tn).reshape(DB * Tn, D_RNN)
            pconv.append(c_p); ph.append(h_p.reshape(B, D_RNN))
            sconv.append(from_tm(c_s, CONV_W - 1)); sh.append(h_s)
            w_proj = w_rec_out[r].astype(BF16)
        else:
            a = i // N_MIXERS
            lam_init = 0.8 - 0.6 * math.exp(-0.3 * i)
            lams = (row(lambda_q1[a]), row(lambda_k1[a]), row(lambda_q2[a]), row(lambda_k2[a]))
            wq = w_qkv[a].astype(BF16)
            q_p, kt_all, v_all, ktb, vb = _qkv_prompt(hp, g_mix, wq, *tabs_p, kt_all, v_all,
                                                      layer=a, batch=B, seq=T)
            q_s, k_s, v_s = _qkv_sample(hs, g_mix, wq, *tabs_s)
            seq3 = lambda t: t.reshape(DB, Tn, D_MODEL)
            ys, yp = _attention(page_table, *lams, row(subln_g[a]), seq3(q_s), seq3(k_s),
                                v_s.reshape(DB, Tn, N_HEADS, V_DIM), cache_k, cache_v, q_p, ktb, vb,
                                layer=a, lam_init=lam_init, batch=B, seq=T)
            ys = ys.reshape(DB * Tn, D_MODEL)
            sk.append(k_s.reshape(DB, Tn, N_HEADS, 2, HEAD_DIM)); sv.append(v_s.reshape(DB, Tn, N_HEADS, V_DIM))
            w_proj = w_attn_out[a].astype(BF16)

        final = i == DEPTH - 1
        post_w = (w_proj, row(norm_ffn[i]), w_ffn_in[i].astype(BF16), w_ffn_out[i].astype(BF16), row(norm_final))
        hp = _post(hp, yp, *post_w, tm=ROW_TILE, final=final)
        hs = _post(hs, ys, *post_w, tm=DB * Tn, final=final)

    prompt_k = jnp.transpose(kt_all.reshape(N_ATTN_LAYERS, B, N_HEADS, 2, HEAD_DIM, T), (0, 1, 5, 2, 3, 4))
    return (hp.reshape(B, T, D_MODEL), hs.reshape(DB, Tn, D_MODEL),
            prompt_k, v_all, jnp.stack(pconv), jnp.stack(ph),
            jnp.stack(sk), jnp.stack(sv), jnp.stack(sconv), jnp.stack(sh))
```

```python
import functools
import math

import jax
import jax.numpy as jnp
from jax import lax
from jax.experimental import pallas as pl
from jax.experimental.pallas import tpu as pltpu

F32 = jnp.float32
BF16 = jnp.bfloat16

D_MODEL = 1024
DEPTH = 4
N_MIXERS = 2
N_ATTN_LAYERS = DEPTH // N_MIXERS
D_RNN = D_MODEL
N_RNN_BLOCKS = 4
RNN_BLOCK = D_RNN // N_RNN_BLOCKS
CONV_W = 4
LRU_C = 8.0
HEAD_DIM = 64
N_HEADS = D_MODEL // (2 * HEAD_DIM)
V_DIM = 2 * HEAD_DIM
ROT_DIM = HEAD_DIM // 4
ROPE_THETA = 500000.0
D_FF = -(-8 * D_MODEL // (3 * 256)) * 256
PAGE_SIZE = 128
NORM_EPS = 1e-6
SUBLN_EPS = 1e-5
NEG_INF = -1e30
QK_SCALE = HEAD_DIM ** -0.5 * math.log2(math.e)

V7X_VMEM_BYTES = 64 * 1024 * 1024
V7X_SUBLANES = 8
V7X_LANES = 128
V7X_MXU_DIM = 256

ROW_TILE = 512
FF_CHUNK = V7X_MXU_DIM
N_FF_CHUNKS = D_FF // FF_CHUNK
ATTN_TILE = 512
PAGES_PER_STEP = 16
VMEM_LIMIT = V7X_VMEM_BYTES - 8 * 1024 * 1024


def _cparams(semantics):
    return pltpu.CompilerParams(dimension_semantics=semantics, vmem_limit_bytes=VMEM_LIMIT)


def _const_spec(shape):
    nd = len(shape)
    return pl.BlockSpec(shape, lambda *_: (0,) * nd, pipeline_mode=pl.Buffered(1))


def _layer_spec(shape, layer):
    nd = len(shape)
    return pl.BlockSpec((None,) + tuple(shape), lambda *_: (layer,) + (0,) * nd, pipeline_mode=pl.Buffered(1))


def _rms(x, g, eps):
    ms = jnp.mean(x * x, axis=-1, keepdims=True)
    return x * lax.rsqrt(ms + eps) * g


def _log_sigmoid(x):
    return jnp.minimum(x, 0.0) - jnp.log1p(jnp.exp(-jnp.abs(x)))


def _lru_coeffs(xc, wx_ref, bx, wa_ref, ba, lam):
    xcb = xc.astype(BF16)

    def gate(w_ref, b):
        z = [jnp.dot(xcb[:, n * RNN_BLOCK:(n + 1) * RNN_BLOCK], w_ref[n], preferred_element_type=F32)
             for n in range(N_RNN_BLOCKS)]
        return jax.nn.sigmoid(jnp.concatenate(z, axis=-1) + b)

    gx = gate(wx_ref, bx)
    ga = gate(wa_ref, ba)
    log_a = ga * (LRU_C * _log_sigmoid(lam))
    a = jnp.exp(log_a)
    mult = jnp.sqrt(jnp.maximum(-jnp.tanh(log_a) * (a * a + 1.0), 0.0))
    return a, mult * gx * xc


def _rec_weight_specs(layer):
    return [_const_spec((1, D_MODEL)), _layer_spec((D_MODEL, 2 * D_RNN), layer),
            _const_spec((CONV_W, D_RNN)), _const_spec((1, D_RNN)),
            _layer_spec((N_RNN_BLOCKS, RNN_BLOCK, RNN_BLOCK), layer), _const_spec((1, D_RNN)),
            _layer_spec((N_RNN_BLOCKS, RNN_BLOCK, RNN_BLOCK), layer), _const_spec((1, D_RNN)),
            _const_spec((1, D_RNN))]


def _rec_prompt_kernel(x_ref, g_ref, win_ref, cw_ref, cb_ref, wx_ref, bx_ref, wa_ref, ba_ref, lam_ref,
                       y_ref, conv_ref, h_ref,
                       xe_ref, a_ref, u_ref, hs_ref, hc_ref, *, tm):
    ti = pl.program_id(1)
    pad = V7X_SUBLANES

    @pl.when(ti == 0)
    def _():
        xe_ref[...] = jnp.zeros_like(xe_ref)
        hc_ref[...] = jnp.zeros_like(hc_ref)

    n = _rms(x_ref[...], g_ref[...], NORM_EPS).astype(BF16)
    y = jnp.dot(n, win_ref[...], preferred_element_type=F32)
    xr = y[:, :D_RNN]
    gate = y[:, D_RNN:]

    cw = cw_ref[...]
    prev = xe_ref[...]
    row = lax.broadcasted_iota(jnp.int32, (pad, D_RNN), 0)
    xc = cb_ref[...]
    for k in range(CONV_W):
        shift = CONV_W - 1 - k
        if shift == 0:
            xs = xr
        else:
            rolled = pltpu.roll(xr, shift, 0)
            head = jnp.where(row < shift, pltpu.roll(prev, shift, 0), rolled[0:pad, :])
            xs = jnp.concatenate([head, rolled[pad:, :]], axis=0)
        xc = xc + xs * cw[k:k + 1, :]
    xe_ref[...] = xr[tm - pad:tm, :]

    a, u = _lru_coeffs(xc, wx_ref, bx_ref[...], wa_ref, ba_ref[...], lam_ref[...])
    a_ref[...] = a
    u_ref[...] = u

    def group(gidx, h):
        r0 = pl.multiple_of(gidx * pad, pad)
        ag = a_ref[pl.ds(r0, pad), :]
        ug = u_ref[pl.ds(r0, pad), :]
        for k in (1, 2, 4):
            m = row >= k
            a_s = jnp.where(m, pltpu.roll(ag, k, 0), 1.0)
            u_s = jnp.where(m, pltpu.roll(ug, k, 0), 0.0)
            ug = ug + ag * u_s
            ag = ag * a_s
        hrows = ag * h + ug
        hs_ref[pl.ds(r0, pad), :] = hrows
        return hrows[pad - 1:pad, :]

    h_last = lax.fori_loop(0, tm // pad, group, hc_ref[...])
    hc_ref[...] = h_last

    y_ref[...] = (hs_ref[...] * jax.nn.gelu(gate)).astype(y_ref.dtype)

    @pl.when(ti == pl.num_programs(1) - 1)
    def _():
        conv_ref[0] = xe_ref[pad - (CONV_W - 1):pad, :]
        h_ref[0] = h_last


def _rec_prompt(x, g, win, cw, cb, wx, bx, wa, ba, lam, *, layer, batch, seq):
    tm = ROW_TILE
    nt = seq // tm
    row_spec = pl.BlockSpec((tm, D_MODEL), lambda b, t: (b * nt + t, 0))
    return pl.pallas_call(
        functools.partial(_rec_prompt_kernel, tm=tm),
        grid=(batch, nt),
        in_specs=[row_spec] + _rec_weight_specs(layer),
        out_specs=[row_spec,
                   pl.BlockSpec((1, CONV_W - 1, D_RNN), lambda b, t: (b, 0, 0)),
                   pl.BlockSpec((1, 1, D_RNN), lambda b, t: (b, 0, 0))],
        out_shape=[jax.ShapeDtypeStruct((batch * seq, D_RNN), BF16),
                   jax.ShapeDtypeStruct((batch, CONV_W - 1, D_RNN), F32),
                   jax.ShapeDtypeStruct((batch, 1, D_RNN), F32)],
        scratch_shapes=[pltpu.VMEM((V7X_SUBLANES, D_RNN), F32),
                        pltpu.VMEM((tm, D_RNN), F32), pltpu.VMEM((tm, D_RNN), F32),
                        pltpu.VMEM((tm, D_RNN), F32), pltpu.VMEM((1, D_RNN), F32)],
        compiler_params=_cparams(("arbitrary", "arbitrary")),
        name="rec_prompt",
    )(x, g, win, cw, cb, wx, bx, wa, ba, lam)


def _rec_sample_kernel(x_ref, sc_ref, h0_ref, g_ref, win_ref, cw_ref, cb_ref, wx_ref, bx_ref, wa_ref, ba_ref,
                       lam_ref, y_ref, conv_ref, h_ref, *, nb, nt):
    n = _rms(x_ref[...], g_ref[...], NORM_EPS).astype(BF16)
    y = jnp.dot(n, win_ref[...], preferred_element_type=F32)
    gate = y[:, D_RNN:]

    nbuf = CONV_W - 1
    xe = [sc_ref[k * nb:(k + 1) * nb, :] for k in range(nbuf)]
    xe += [y[t * nb:(t + 1) * nb, :D_RNN] for t in range(nt)]
    cw = cw_ref[...]
    slabs = []
    for t in range(nt):
        xc = cb_ref[...]
        for k in range(CONV_W):
            xc = xc + xe[t + k] * cw[k:k + 1, :]
        slabs.append(xc)
    xc = jnp.concatenate(slabs, axis=0)

    a, u = _lru_coeffs(xc, wx_ref, bx_ref[...], wa_ref, ba_ref[...], lam_ref[...])
    h = h0_ref[...]
    for t in range(nt):
        rows = slice(t * nb, (t + 1) * nb)
        h = a[rows] * h + u[rows]
        y_ref[rows, :] = h * jax.nn.gelu(gate[rows])
    h_ref[...] = h
    for k in range(nbuf):
        conv_ref[k * nb:(k + 1) * nb, :] = xe[len(xe) - nbuf + k]


def _rec_sample(x, sc, h0, g, win, cw, cb, wx, bx, wa, ba, lam, *, layer, nb, nt):
    m = nb * nt
    nbuf = CONV_W - 1
    return pl.pallas_call(
        functools.partial(_rec_sample_kernel, nb=nb, nt=nt),
        grid=(1,),
        in_specs=[_const_spec((m, D_MODEL)), _const_spec((nb * nbuf, D_RNN)), _const_spec((nb, D_RNN))]
                 + _rec_weight_specs(layer),
        out_specs=[pl.BlockSpec(s, lambda i: (0, 0)) for s in ((m, D_RNN), (nb * nbuf, D_RNN), (nb, D_RNN))],
        out_shape=[jax.ShapeDtypeStruct((m, D_RNN), F32),
                   jax.ShapeDtypeStruct((nb * nbuf, D_RNN), F32),
                   jax.ShapeDtypeStruct((nb, D_RNN), F32)],
        compiler_params=_cparams(("arbitrary",)),
        name="rec_sample",
    )(x, sc, h0, g, win, cw, cb, wx, bx, wa, ba, lam)


def _rope(xh, c, sneg, spos):
    half = ROT_DIM // 2
    return xh * c + pltpu.roll(xh, V7X_LANES - half, 1) * sneg + pltpu.roll(xh, half, 1) * spos


def _rope_tables(pos):
    half = ROT_DIM // 2
    inv = jnp.power(ROPE_THETA, -jnp.arange(half, dtype=F32) * 2.0 / ROT_DIM)
    ang = pos[:, None] * inv[None, :]
    cos, sin = jnp.cos(ang), jnp.sin(ang)
    n = pos.shape[0]
    ones = jnp.ones((n, HEAD_DIM - ROT_DIM), F32)
    zeros = jnp.zeros((n, HEAD_DIM - ROT_DIM), F32)
    zh = jnp.zeros((n, half), F32)
    c = jnp.concatenate([cos, cos, ones], axis=-1)
    sneg = jnp.concatenate([-sin, zh, zeros], axis=-1)
    spos = jnp.concatenate([zh, sin, zeros], axis=-1)
    rep = V7X_LANES // HEAD_DIM
    return tuple(jnp.tile(t, (1, rep)) for t in (c, sneg, spos))


def _qkv_sample_kernel(x_ref, g_ref, w_ref, cos_ref, sneg_ref, spos_ref, q_ref, k_ref, v_ref):
    n = _rms(x_ref[...], g_ref[...], NORM_EPS).astype(BF16)
    qkv = jnp.dot(n, w_ref[...], preferred_element_type=F32)
    tabs = (cos_ref[...], sneg_ref[...], spos_ref[...])
    for h in range(N_HEADS):
        lo, hi = h * V_DIM, (h + 1) * V_DIM
        q_ref[:, lo:hi] = _rope(qkv[:, lo:hi], *tabs) * QK_SCALE
        k_ref[:, lo:hi] = _rope(qkv[:, D_MODEL + lo:D_MODEL + hi], *tabs)
    v_ref[...] = qkv[:, 2 * D_MODEL:]


def _qkv_sample(x, g, w, cos, sneg, spos, *, layer):
    m = x.shape[0]
    out = jax.ShapeDtypeStruct((m, D_MODEL), F32)
    row_spec = _const_spec((m, D_MODEL))
    tab_spec = _const_spec((m, V7X_LANES))
    return pl.pallas_call(
        _qkv_sample_kernel,
        grid=(1,),
        in_specs=[row_spec, _const_spec((1, D_MODEL)), _layer_spec((D_MODEL, 3 * D_MODEL), layer),
                  tab_spec, tab_spec, tab_spec],
        out_specs=[pl.BlockSpec((m, D_MODEL), lambda i: (0, 0))] * 3,
        out_shape=[out, out, out],
        compiler_params=_cparams(("arbitrary",)),
        name="qkv_sample",
    )(x, g, w, cos, sneg, spos)


def _qkv_prompt_kernel(x_ref, g_ref, w_ref, cos_ref, sneg_ref, spos_ref, *rest):
    q_ref, kt_ref, vo_ref, ktb_ref, vb_ref = rest[-5:]
    n = _rms(x_ref[...], g_ref[...], NORM_EPS).astype(BF16)
    qkv = jnp.dot(n, w_ref[...], preferred_element_type=F32)
    tabs = (cos_ref[...], sneg_ref[...], spos_ref[...])
    for h in range(N_HEADS):
        lo, hi = h * V_DIM, (h + 1) * V_DIM
        q_ref[h] = (_rope(qkv[:, lo:hi], *tabs) * QK_SCALE).astype(q_ref.dtype)
        kt = _rope(qkv[:, D_MODEL + lo:D_MODEL + hi], *tabs).T
        kt_ref[lo:hi, :] = kt
        ktb_ref[lo:hi, :] = kt.astype(BF16)
        vh = qkv[:, 2 * D_MODEL + lo:2 * D_MODEL + hi]
        vo_ref[:, h, :] = vh
        vb_ref[h] = vh.astype(BF16)


def _qkv_prompt(x, g, w, cos, sneg, spos, kt_all, v_all, *, layer, batch, seq):
    tm = ROW_TILE
    nt = seq // tm
    row_spec = pl.BlockSpec((tm, D_MODEL), lambda b, t: (b * nt + t, 0))
    tab_spec = pl.BlockSpec((tm, V7X_LANES), lambda b, t: (t, 0))
    in_specs = [row_spec, _const_spec((1, D_MODEL)), _layer_spec((D_MODEL, 3 * D_MODEL), layer),
                tab_spec, tab_spec, tab_spec]
    args = [x, g, w, cos, sneg, spos]
    aliases = {}
    if kt_all is not None:
        in_specs += [pl.BlockSpec(memory_space=pl.ANY)] * 2
        aliases = {len(args): 1, len(args) + 1: 2}
        args += [kt_all, v_all]
    return pl.pallas_call(
        _qkv_prompt_kernel,
        grid=(batch, nt),
        in_specs=in_specs,
        out_specs=[pl.BlockSpec((None, N_HEADS, tm, V_DIM), lambda b, t: (b, 0, t, 0)),
                   pl.BlockSpec((None, None, D_MODEL, tm), lambda b, t: (layer, b, 0, t)),
                   pl.BlockSpec((None, None, tm, N_HEADS, V_DIM), lambda b, t: (layer, b, t, 0, 0)),
                   pl.BlockSpec((None, D_MODEL, tm), lambda b, t: (b, 0, t)),
                   pl.BlockSpec((None, N_HEADS, tm, V_DIM), lambda b, t: (b, 0, t, 0))],
        out_shape=[jax.ShapeDtypeStruct((batch, N_HEADS, seq, V_DIM), BF16),
                   jax.ShapeDtypeStruct((N_ATTN_LAYERS, batch, D_MODEL, seq), F32),
                   jax.ShapeDtypeStruct((N_ATTN_LAYERS, batch, seq, N_HEADS, V_DIM), F32),
                   jax.ShapeDtypeStruct((batch, D_MODEL, seq), BF16),
                   jax.ShapeDtypeStruct((batch, N_HEADS, seq, V_DIM), BF16)],
        input_output_aliases=aliases,
        compiler_params=_cparams(("arbitrary", "arbitrary")),
        name="qkv_prompt",
    )(*args)


def _diff_lambda(lq1, lk1, lq2, lk2, lam_init):
    s1 = jnp.sum(lq1 * lk1, axis=-1, keepdims=True)
    s2 = jnp.sum(lq2 * lk2, axis=-1, keepdims=True)
    return jnp.exp(s1) - jnp.exp(s2) + lam_init


def _attn_kernel(qi_ref, ki_ref, lq1_ref, lk1_ref, lq2_ref, lk2_ref, sg_ref, q_ref, kt_ref, v_ref, o_ref,
                 q2_ref, m_ref, acc_ref, oh_ref, *, tile, lam_init):
    step = pl.program_id(1)
    qi = qi_ref[step]
    ki = ki_ref[step]

    @pl.when(ki == 0)
    def _():
        q = q_ref[...]
        lane = lax.broadcasted_iota(jnp.int32, q.shape, 2)
        zero = jnp.zeros_like(q)
        q2_ref[:, 0:tile, :] = jnp.where(lane < HEAD_DIM, q, zero)
        q2_ref[:, tile:2 * tile, :] = jnp.where(lane >= HEAD_DIM, q, zero)
        m_ref[...] = jnp.full_like(m_ref, NEG_INF)
        acc_ref[...] = jnp.zeros_like(acc_ref)

    def scores(h):
        kt = kt_ref[pl.ds(pl.multiple_of(h * V_DIM, V_DIM), V_DIM), :]
        return jnp.dot(q2_ref[h], kt, preferred_element_type=F32)

    def update(h, s, diagonal):
        if diagonal:
            row = lax.broadcasted_iota(jnp.int32, s.shape, 0)
            col = lax.broadcasted_iota(jnp.int32, s.shape, 1)
            s = jnp.where(col <= jnp.where(row >= tile, row - tile, row), s, NEG_INF)
        m_prev = m_ref[h]
        m_new = jnp.maximum(m_prev, jnp.max(s, axis=-1, keepdims=True))
        alpha = jnp.exp2(m_prev - m_new)
        p = jnp.exp2(s - jnp.tile(m_new, (1, tile // V7X_LANES))).astype(BF16)
        v1 = jnp.concatenate([v_ref[h], jnp.ones((tile, V7X_LANES), BF16)], axis=1)
        acc_ref[h] = jnp.tile(alpha, (1, 2)) * acc_ref[h] + jnp.dot(p, v1, preferred_element_type=F32)
        m_ref[h] = m_new

    @pl.when(ki < qi)
    def _():
        def head(h, carry):
            update(h, scores(h), False)
            return carry
        lax.fori_loop(0, N_HEADS, head, 0)

    @pl.when(ki == qi)
    def _():
        lam = _diff_lambda(lq1_ref[...], lk1_ref[...], lq2_ref[...], lk2_ref[...], lam_init)

        def head(h, carry):
            update(h, scores(h), True)
            acc = acc_ref[h]
            o = acc[:, 0:V_DIM] / acc[:, V_DIM:2 * V_DIM]
            o = o[0:tile, :] - lam * o[tile:2 * tile, :]
            oh_ref[h] = (_rms(o, sg_ref[...], SUBLN_EPS) * (1.0 - lam_init)).astype(oh_ref.dtype)
            return carry
        lax.fori_loop(0, N_HEADS, head, 0)
        for h in range(N_HEADS):
            o_ref[:, h * V_DIM:(h + 1) * V_DIM] = oh_ref[h]


def _attn_prompt(lq1, lk1, lq2, lk2, sg, q, ktb, vb, *, batch, seq, lam_init):
    tile = ATTN_TILE
    nq = seq // tile
    pairs = [(i, j) for i in range(nq) for j in range(i + 1)]
    qi_tab = jnp.array([i for i, _ in pairs], jnp.int32)
    ki_tab = jnp.array([j for _, j in pairs], jnp.int32)
    const2 = lambda shape: pl.BlockSpec(shape, lambda b, s, qt, kt: (0, 0))
    lam_spec = const2((1, HEAD_DIM))
    q_spec = pl.BlockSpec((None, N_HEADS, tile, V_DIM), lambda b, s, qt, kt: (b, 0, qt[s], 0))
    kt_spec = pl.BlockSpec((None, D_MODEL, tile), lambda b, s, qt, kt: (b, 0, kt[s]))
    v_spec = pl.BlockSpec((None, N_HEADS, tile, V_DIM), lambda b, s, qt, kt: (b, 0, kt[s], 0))
    grid_spec = pltpu.PrefetchScalarGridSpec(
        num_scalar_prefetch=2,
        grid=(batch, len(pairs)),
        in_specs=[lam_spec, lam_spec, lam_spec, lam_spec, const2((1, V_DIM)), q_spec, kt_spec, v_spec],
        out_specs=pl.BlockSpec((tile, D_MODEL), lambda b, s, qt, kt: (b * nq + qt[s], 0)),
        scratch_shapes=[pltpu.VMEM((N_HEADS, 2 * tile, V_DIM), BF16),
                        pltpu.VMEM((N_HEADS, 2 * tile, V7X_LANES), F32),
                        pltpu.VMEM((N_HEADS, 2 * tile, 2 * V_DIM), F32),
                        pltpu.VMEM((N_HEADS, tile, V_DIM), BF16)],
    )
    return pl.pallas_call(
        functools.partial(_attn_kernel, tile=tile, lam_init=lam_init),
        grid_spec=grid_spec,
        out_shape=jax.ShapeDtypeStruct((batch * seq, D_MODEL), BF16),
        compiler_params=_cparams(("arbitrary", "arbitrary")),
        name="attn_prompt",
    )(qi_tab, ki_tab, lq1, lk1, lq2, lk2, sg, q, ktb, vb)


def _attn_sample_kernel(pt_ref, lq1_ref, lk1_ref, lq2_ref, lk2_ref, sg_ref, e_ref, q_ref, kn_ref, vn_ref, *rest,
                        nt, lam_init):
    del pt_ref
    npp = PAGES_PER_STEP
    k_refs, v_refs = rest[:npp], rest[npp:2 * npp]
    o_ref = rest[2 * npp]
    qt_ref, qtb_ref, m_ref, l_ref, acc_ref = rest[2 * npp + 1:]
    j = pl.program_id(1)
    nrow = nt * N_HEADS

    @pl.when(j == 0)
    def _():
        q = q_ref[0]
        sub = lax.broadcasted_iota(jnp.int32, (N_HEADS, D_MODEL), 0)
        col = lax.broadcasted_iota(jnp.int32, (N_HEADS, D_MODEL), 1)
        for c in range(2):
            keep = (col // V_DIM == sub) & ((col // HEAD_DIM) % 2 == c)
            for t in range(nt):
                r0 = c * nrow + t * N_HEADS
                qt_ref[r0:r0 + N_HEADS, :] = jnp.where(keep, jnp.broadcast_to(q[t:t + 1, :], (N_HEADS, D_MODEL)), 0.0)
        qtb_ref[...] = qt_ref[...].astype(BF16)

        qt = qt_ref[...]
        kn = kn_ref[0]
        r = lax.broadcasted_iota(jnp.int32, (2 * nrow, 1), 0)
        t_row = (r // N_HEADS) % nt
        sn = []
        for t in range(nt):
            st = jnp.sum(qt * kn[t:t + 1, :], axis=-1, keepdims=True)
            sn.append(jnp.where(t <= t_row, st, NEG_INF))
        m = sn[0]
        for st in sn[1:]:
            m = jnp.maximum(m, st)
        l = jnp.zeros_like(m)
        acc = jnp.zeros(acc_ref.shape, F32)
        for t in range(nt):
            pt = jnp.exp2(sn[t] - m)
            l = l + pt
            vt = jnp.concatenate([vn_ref[0, t]] * (2 * nt), axis=0)
            acc = acc + pt * vt
        m_ref[...] = m
        l_ref[...] = l
        acc_ref[...] = acc

    qtb = qtb_ref[...]
    s = jnp.concatenate([jnp.dot(qtb, k_refs[i][...].astype(BF16), preferred_element_type=F32)
                         for i in range(npp)], axis=1)
    m_prev = m_ref[...]
    m_new = jnp.maximum(m_prev, jnp.max(s, axis=-1, keepdims=True))
    alpha = jnp.exp2(m_prev - m_new)
    p = jnp.exp2(s - m_new)
    l_ref[...] = alpha * l_ref[...] + jnp.sum(p, axis=-1, keepdims=True)
    m_ref[...] = m_new
    pb = p.astype(BF16)
    pstack = jnp.concatenate([pb[:, i * PAGE_SIZE:(i + 1) * PAGE_SIZE] for i in range(npp)], axis=0)
    pe = jnp.dot(pstack, e_ref[...], preferred_element_type=F32)
    hh = lax.broadcasted_iota(jnp.int32, (N_HEADS, pe.shape[1]), 0)
    cc = lax.broadcasted_iota(jnp.int32, (N_HEADS, pe.shape[1]), 1)
    own = (hh == cc % N_HEADS)[None]
    pe = jnp.where(own, pe.reshape(pe.shape[0] // N_HEADS, N_HEADS, pe.shape[1]), 0.0).reshape(pe.shape)
    pe = pe.astype(BF16)
    acc = alpha * acc_ref[...]
    for i in range(npp):
        vb = v_refs[i][...].astype(BF16)
        acc = acc + jnp.dot(pe[i * 2 * nrow:(i + 1) * 2 * nrow, :], vb, preferred_element_type=F32)
    acc_ref[...] = acc

    @pl.when(j == pl.num_programs(1) - 1)
    def _():
        acc = acc_ref[...] / l_ref[...]
        lam = _diff_lambda(lq1_ref[...], lk1_ref[...], lq2_ref[...], lk2_ref[...], lam_init)
        o = acc[0:nrow, :] - lam * acc[nrow:2 * nrow, :]
        o = _rms(o, sg_ref[...], SUBLN_EPS) * (1.0 - lam_init)
        for t in range(nt):
            o_ref[0, t] = o[t * N_HEADS:(t + 1) * N_HEADS, :]


def _attn_sample(page_table, lq1, lk1, lq2, lk2, sg, q, kn, vn, cache_k, cache_v, *, layer, lam_init):
    nb, nt, _ = q.shape
    n_layers, n_pool = cache_k.shape[:2]
    n_pages = page_table.shape[1]
    npp = PAGES_PER_STEP
    nj = n_pages // npp
    nrow2 = 2 * nt * N_HEADS
    vrows = PAGE_SIZE * N_HEADS

    def page_map(i):
        return lambda b, j, pt: (layer, pt[b, j * npp + i], 0, 0)

    const2 = lambda shape: pl.BlockSpec(shape, lambda b, j, pt: (0, 0))
    seq_spec = pl.BlockSpec((1, nt, D_MODEL), lambda b, j, pt: (b, 0, 0))
    head_spec = pl.BlockSpec((1, nt, N_HEADS, V_DIM), lambda b, j, pt: (b, 0, 0, 0))
    lam_spec = const2((1, HEAD_DIM))
    grid_spec = pltpu.PrefetchScalarGridSpec(
        num_scalar_prefetch=1,
        grid=(nb, nj),
        in_specs=[lam_spec, lam_spec, lam_spec, lam_spec, const2((1, V_DIM)), const2((PAGE_SIZE, vrows)),
                  seq_spec, seq_spec, head_spec]
                 + [pl.BlockSpec((None, None, D_MODEL, PAGE_SIZE), page_map(i)) for i in range(npp)]
                 + [pl.BlockSpec((None, None, vrows, V_DIM), page_map(i)) for i in range(npp)],
        out_specs=head_spec,
        scratch_shapes=[pltpu.VMEM((nrow2, D_MODEL), F32), pltpu.VMEM((nrow2, D_MODEL), BF16),
                        pltpu.VMEM((nrow2, 1), F32), pltpu.VMEM((nrow2, 1), F32),
                        pltpu.VMEM((nrow2, V_DIM), F32)],
    )
    ck = jnp.transpose(cache_k, (0, 1, 3, 4, 5, 2)).reshape(n_layers, n_pool, D_MODEL, PAGE_SIZE)
    cv = cache_v.reshape(n_layers, n_pool, vrows, V_DIM)
    spread = (jnp.arange(vrows)[None, :] // N_HEADS == jnp.arange(PAGE_SIZE)[:, None]).astype(BF16)
    return pl.pallas_call(
        functools.partial(_attn_sample_kernel, nt=nt, lam_init=lam_init),
        grid_spec=grid_spec,
        out_shape=jax.ShapeDtypeStruct((nb, nt, N_HEADS, V_DIM), F32),
        compiler_params=_cparams(("arbitrary", "arbitrary")),
        name="attn_sample",
    )(page_table, lq1, lk1, lq2, lk2, sg, spread, q, kn, vn, *([ck] * npp), *([cv] * npp))


def _post_kernel(x_ref, y_ref, wp_ref, g_ref, wi_ref, wo_ref, gf_ref, o_ref, *, final):
    x = x_ref[...] + jnp.dot(y_ref[...].astype(BF16), wp_ref[...], preferred_element_type=F32)
    n = _rms(x, g_ref[...], NORM_EPS).astype(BF16)
    out = x
    for c in range(N_FF_CHUNKS):
        lo, hi = c * FF_CHUNK, (c + 1) * FF_CHUNK
        gc = jnp.dot(n, wi_ref[:, lo:hi], preferred_element_type=F32)
        uc = jnp.dot(n, wi_ref[:, D_FF + lo:D_FF + hi], preferred_element_type=F32)
        act = (gc * jax.nn.sigmoid(gc) * uc).astype(BF16)
        out = out + jnp.dot(act, wo_ref[lo:hi, :], preferred_element_type=F32)
    if final:
        out = _rms(out, gf_ref[...], NORM_EPS)
    o_ref[...] = out


def _post(x, y, wp, g, wi, wo, gf, *, tm, final, proj_layer, ffn_layer):
    m = x.shape[0]
    row_spec = pl.BlockSpec((tm, D_MODEL), lambda i: (i, 0))
    return pl.pallas_call(
        functools.partial(_post_kernel, final=final),
        grid=(m // tm,),
        in_specs=[row_spec, row_spec, _layer_spec((D_MODEL, D_MODEL), proj_layer), _const_spec((1, D_MODEL)),
                  _layer_spec((D_MODEL, 2 * D_FF), ffn_layer), _layer_spec((D_FF, D_MODEL), ffn_layer),
                  _const_spec((1, D_MODEL))],
        out_specs=row_spec,
        out_shape=jax.ShapeDtypeStruct((m, D_MODEL), F32),
        compiler_params=_cparams(("arbitrary",)),
        name="post",
    )(x, y, wp, g, wi, wo, gf)


def kernel(x_prompt, x_sample, cache_k, cache_v, page_table, state_conv, state_h, norm_mix, w_rec_in, conv_w,
           conv_b, gate_x_w, gate_x_b, gate_a_w, gate_a_b, lru_lambda, w_rec_out, w_qkv, lambda_q1, lambda_k1,
           lambda_q2, lambda_k2, subln_g, w_attn_out, norm_ffn, w_ffn_in, w_ffn_out, norm_final):
    B, T, _ = x_prompt.shape
    DB, Tn, _ = x_sample.shape
    past = page_table.shape[1] * PAGE_SIZE
    row = lambda a: a.reshape(1, -1)

    hp = x_prompt.reshape(B * T, D_MODEL)
    hs = x_sample.reshape(DB * Tn, D_MODEL)
    tabs_p = _rope_tables(jnp.arange(T, dtype=F32))
    tabs_s = _rope_tables(jnp.tile(past + jnp.arange(Tn, dtype=F32), DB))

    w_rec_in_b, gate_x_b16, gate_a_b16 = w_rec_in.astype(BF16), gate_x_w.astype(BF16), gate_a_w.astype(BF16)
    w_rec_out_b, w_qkv_b, w_attn_out_b = w_rec_out.astype(BF16), w_qkv.astype(BF16), w_attn_out.astype(BF16)
    w_ffn_in_b, w_ffn_out_b = w_ffn_in.astype(BF16), w_ffn_out.astype(BF16)

    kt_all = v_all = None
    pconv, ph = [], []
    sk, sv, sconv, sh = [], [], [], []
    for i in range(DEPTH):
        g_mix = row(norm_mix[i])
        if i % N_MIXERS == 0:
            r = i // N_MIXERS
            w = (w_rec_in_b, conv_w[r], row(conv_b[r]), gate_x_b16, row(gate_x_b[r]), gate_a_b16, row(gate_a_b[r]),
                 row(lru_lambda[r]))
            yp, c_p, h_p = _rec_prompt(hp, g_mix, *w, layer=r, batch=B, seq=T)
            to_tm = lambda t, n: t.reshape(DB, n, D_RNN).swapaxes(0, 1).reshape(n * DB, D_RNN)
            from_tm = lambda t, n: t.reshape(n, DB, D_RNN).swapaxes(0, 1)
            ys, c_s, h_s = _rec_sample(to_tm(hs, Tn), to_tm(state_conv[r], CONV_W - 1), state_h[r], g_mix, *w,
                                       layer=r, nb=DB, nt=Tn)
            ys = from_tm(ys, Tn).reshape(DB * Tn, D_RNN)
            pconv.append(c_p); ph.append(h_p.reshape(B, D_RNN))
            sconv.append(from_tm(c_s, CONV_W - 1)); sh.append(h_s)
            w_proj, proj_layer = w_rec_out_b, r
        else:
            a = i // N_MIXERS
            lam_init = 0.8 - 0.6 * math.exp(-0.3 * i)
            lams = (row(lambda_q1[a]), row(lambda_k1[a]), row(lambda_q2[a]), row(lambda_k2[a]))
            q_p, kt_all, v_all, ktb, vb = _qkv_prompt(hp, g_mix, w_qkv_b, *tabs_p, kt_all, v_all,
                                                      layer=a, batch=B, seq=T)
            yp = _attn_prompt(*lams, row(subln_g[a]), q_p, ktb, vb, batch=B, seq=T, lam_init=lam_init)
            q_s, k_s, v_s = _qkv_sample(hs, g_mix, w_qkv_b, *tabs_s, layer=a)
            seq3 = lambda t: t.reshape(DB, Tn, D_MODEL)
            ys = _attn_sample(page_table, *lams, row(subln_g[a]), seq3(q_s), seq3(k_s),
                              v_s.reshape(DB, Tn, N_HEADS, V_DIM), cache_k, cache_v,
                              layer=a, lam_init=lam_init).reshape(DB * Tn, D_MODEL)
            sk.append(k_s.reshape(DB, Tn, N_HEADS, 2, HEAD_DIM)); sv.append(v_s.reshape(DB, Tn, N_HEADS, V_DIM))
            w_proj, proj_layer = w_attn_out_b, a

        post_w = (w_proj, row(norm_ffn[i]), w_ffn_in_b, w_ffn_out_b, row(norm_final))
        post_kw = dict(final=i == DEPTH - 1, proj_layer=proj_layer, ffn_layer=i)
        hp = _post(hp, yp, *post_w, tm=ROW_TILE, **post_kw)
        hs = _post(hs, ys, *post_w, tm=DB * Tn, **post_kw)

    prompt_k = jnp.transpose(kt_all.reshape(N_ATTN_LAYERS, B, N_HEADS, 2, HEAD_DIM, T), (0, 1, 5, 2, 3, 4))
    return (hp.reshape(B, T, D_MODEL), hs.reshape(DB, Tn, D_MODEL),
            prompt_k, v_all, jnp.stack(pconv), jnp.stack(ph),
            jnp.stack(sk), jnp.stack(sv), jnp.stack(sconv), jnp.stack(sh))
```

```python
import functools
import math

import jax
import jax.numpy as jnp
from jax import lax
from jax.experimental import pallas as pl
from jax.experimental.pallas import tpu as pltpu

F32 = jnp.float32
BF16 = jnp.bfloat16

D_MODEL = 1024
DEPTH = 4
N_MIXERS = 2
N_ATTN_LAYERS = DEPTH // N_MIXERS
D_RNN = D_MODEL
N_RNN_BLOCKS = 4
RNN_BLOCK = D_RNN // N_RNN_BLOCKS
CONV_W = 4
LRU_C = 8.0
HEAD_DIM = 64
N_HEADS = D_MODEL // (2 * HEAD_DIM)
V_DIM = 2 * HEAD_DIM
ROT_DIM = HEAD_DIM // 4
ROPE_THETA = 500000.0
D_FF = -(-8 * D_MODEL // (3 * 256)) * 256
PAGE_SIZE = 128
NORM_EPS = 1e-6
SUBLN_EPS = 1e-5
NEG_INF = -1e30
QK_SCALE = HEAD_DIM ** -0.5 * math.log2(math.e)

V7X_VMEM_BYTES = 64 * 1024 * 1024
V7X_SUBLANES = 8
V7X_LANES = 128
V7X_MXU_DIM = 256

ROW_TILE = 512
FF_CHUNK = V7X_MXU_DIM
N_FF_CHUNKS = D_FF // FF_CHUNK
ATTN_TILE = 512
PAGES_PER_STEP = 16
VMEM_LIMIT = V7X_VMEM_BYTES - 8 * 1024 * 1024


def _cparams(semantics):
    return pltpu.CompilerParams(dimension_semantics=semantics, vmem_limit_bytes=VMEM_LIMIT)


def _const_spec(shape):
    nd = len(shape)
    return pl.BlockSpec(shape, lambda *_: (0,) * nd, pipeline_mode=pl.Buffered(1))


def _layer_spec(shape, layer):
    nd = len(shape)
    return pl.BlockSpec((None,) + tuple(shape), lambda *_: (layer,) + (0,) * nd, pipeline_mode=pl.Buffered(1))


def _rms(x, g, eps):
    ms = jnp.mean(x * x, axis=-1, keepdims=True)
    return x * lax.rsqrt(ms + eps) * g


def _log_sigmoid(x):
    return jnp.minimum(x, 0.0) - jnp.log1p(jnp.exp(-jnp.abs(x)))


def _lru_coeffs(xc, wx_ref, bx, wa_ref, ba, lam):
    xcb = xc.astype(BF16)

    def gate(w_ref, b):
        z = [jnp.dot(xcb[:, n * RNN_BLOCK:(n + 1) * RNN_BLOCK], w_ref[n], preferred_element_type=F32)
             for n in range(N_RNN_BLOCKS)]
        return jax.nn.sigmoid(jnp.concatenate(z, axis=-1) + b)

    gx = gate(wx_ref, bx)
    ga = gate(wa_ref, ba)
    log_a = ga * (LRU_C * _log_sigmoid(lam))
    a = jnp.exp(log_a)
    mult = jnp.sqrt(jnp.maximum(-jnp.tanh(log_a) * (a * a + 1.0), 0.0))
    return a, mult * gx * xc


def _rec_weight_specs(layer):
    return [_const_spec((1, D_MODEL)), _layer_spec((D_MODEL, 2 * D_RNN), layer),
            _const_spec((CONV_W, D_RNN)), _const_spec((1, D_RNN)),
            _layer_spec((N_RNN_BLOCKS, RNN_BLOCK, RNN_BLOCK), layer), _const_spec((1, D_RNN)),
            _layer_spec((N_RNN_BLOCKS, RNN_BLOCK, RNN_BLOCK), layer), _const_spec((1, D_RNN)),
            _const_spec((1, D_RNN))]


def _rec_prompt_kernel(x_ref, g_ref, win_ref, cw_ref, cb_ref, wx_ref, bx_ref, wa_ref, ba_ref, lam_ref,
                       y_ref, conv_ref, h_ref,
                       xe_ref, a_ref, u_ref, hs_ref, hc_ref, *, tm):
    ti = pl.program_id(1)
    pad = V7X_SUBLANES

    @pl.when(ti == 0)
    def _():
        xe_ref[...] = jnp.zeros_like(xe_ref)
        hc_ref[...] = jnp.zeros_like(hc_ref)

    n = _rms(x_ref[...], g_ref[...], NORM_EPS).astype(BF16)
    y = jnp.dot(n, win_ref[...], preferred_element_type=F32)
    xr = y[:, :D_RNN]
    gate = y[:, D_RNN:]

    cw = cw_ref[...]
    prev = xe_ref[...]
    row = lax.broadcasted_iota(jnp.int32, (pad, D_RNN), 0)
    xc = cb_ref[...]
    for k in range(CONV_W):
        shift = CONV_W - 1 - k
        if shift == 0:
            xs = xr
        else:
            rolled = pltpu.roll(xr, shift, 0)
            head = jnp.where(row < shift, pltpu.roll(prev, shift, 0), rolled[0:pad, :])
            xs = jnp.concatenate([head, rolled[pad:, :]], axis=0)
        xc = xc + xs * cw[k:k + 1, :]
    xe_ref[...] = xr[tm - pad:tm, :]

    a, u = _lru_coeffs(xc, wx_ref, bx_ref[...], wa_ref, ba_ref[...], lam_ref[...])
    a_ref[...] = a
    u_ref[...] = u

    def group(gidx, h):
        r0 = pl.multiple_of(gidx * pad, pad)
        ag = a_ref[pl.ds(r0, pad), :]
        ug = u_ref[pl.ds(r0, pad), :]
        for k in (1, 2, 4):
            m = row >= k
            a_s = jnp.where(m, pltpu.roll(ag, k, 0), 1.0)
            u_s = jnp.where(m, pltpu.roll(ug, k, 0), 0.0)
            ug = ug + ag * u_s
            ag = ag * a_s
        hrows = ag * h + ug
        hs_ref[pl.ds(r0, pad), :] = hrows
        return hrows[pad - 1:pad, :]

    h_last = lax.fori_loop(0, tm // pad, group, hc_ref[...])
    hc_ref[...] = h_last

    y_ref[...] = (hs_ref[...] * jax.nn.gelu(gate)).astype(y_ref.dtype)

    @pl.when(ti == pl.num_programs(1) - 1)
    def _():
        conv_ref[0] = xe_ref[pad - (CONV_W - 1):pad, :]
        h_ref[0] = h_last


def _rec_prompt(x, g, win, cw, cb, wx, bx, wa, ba, lam, *, layer, batch, seq):
    tm = ROW_TILE
    nt = seq // tm
    row_spec = pl.BlockSpec((tm, D_MODEL), lambda b, t: (b * nt + t, 0))
    return pl.pallas_call(
        functools.partial(_rec_prompt_kernel, tm=tm),
        grid=(batch, nt),
        in_specs=[row_spec] + _rec_weight_specs(layer),
        out_specs=[row_spec,
                   pl.BlockSpec((1, CONV_W - 1, D_RNN), lambda b, t: (b, 0, 0)),
                   pl.BlockSpec((1, 1, D_RNN), lambda b, t: (b, 0, 0))],
        out_shape=[jax.ShapeDtypeStruct((batch * seq, D_RNN), BF16),
                   jax.ShapeDtypeStruct((batch, CONV_W - 1, D_RNN), F32),
                   jax.ShapeDtypeStruct((batch, 1, D_RNN), F32)],
        scratch_shapes=[pltpu.VMEM((V7X_SUBLANES, D_RNN), F32),
                        pltpu.VMEM((tm, D_RNN), F32), pltpu.VMEM((tm, D_RNN), F32),
                        pltpu.VMEM((tm, D_RNN), F32), pltpu.VMEM((1, D_RNN), F32)],
        compiler_params=_cparams(("arbitrary", "arbitrary")),
        name="rec_prompt",
    )(x, g, win, cw, cb, wx, bx, wa, ba, lam)


def _rec_sample_kernel(x_ref, sc_ref, h0_ref, g_ref, win_ref, cw_ref, cb_ref, wx_ref, bx_ref, wa_ref, ba_ref,
                       lam_ref, y_ref, conv_ref, h_ref, *, nb, nt):
    n = _rms(x_ref[...], g_ref[...], NORM_EPS).astype(BF16)
    y = jnp.dot(n, win_ref[...], preferred_element_type=F32)
    gate = y[:, D_RNN:]

    nbuf = CONV_W - 1
    xe = [sc_ref[k * nb:(k + 1) * nb, :] for k in range(nbuf)]
    xe += [y[t * nb:(t + 1) * nb, :D_RNN] for t in range(nt)]
    cw = cw_ref[...]
    slabs = []
    for t in range(nt):
        xc = cb_ref[...]
        for k in range(CONV_W):
            xc = xc + xe[t + k] * cw[k:k + 1, :]
        slabs.append(xc)
    xc = jnp.concatenate(slabs, axis=0)

    a, u = _lru_coeffs(xc, wx_ref, bx_ref[...], wa_ref, ba_ref[...], lam_ref[...])
    h = h0_ref[...]
    for t in range(nt):
        rows = slice(t * nb, (t + 1) * nb)
        h = a[rows] * h + u[rows]
        y_ref[rows, :] = h * jax.nn.gelu(gate[rows])
    h_ref[...] = h
    for k in range(nbuf):
        conv_ref[k * nb:(k + 1) * nb, :] = xe[len(xe) - nbuf + k]


def _rec_sample(x, sc, h0, g, win, cw, cb, wx, bx, wa, ba, lam, *, layer, nb, nt):
    m = nb * nt
    nbuf = CONV_W - 1
    return pl.pallas_call(
        functools.partial(_rec_sample_kernel, nb=nb, nt=nt),
        grid=(1,),
        in_specs=[_const_spec((m, D_MODEL)), _const_spec((nb * nbuf, D_RNN)), _const_spec((nb, D_RNN))]
                 + _rec_weight_specs(layer),
        out_specs=[pl.BlockSpec(s, lambda i: (0, 0)) for s in ((m, D_RNN), (nb * nbuf, D_RNN), (nb, D_RNN))],
        out_shape=[jax.ShapeDtypeStruct((m, D_RNN), F32),
                   jax.ShapeDtypeStruct((nb * nbuf, D_RNN), F32),
                   jax.ShapeDtypeStruct((nb, D_RNN), F32)],
        compiler_params=_cparams(("arbitrary",)),
        name="rec_sample",
    )(x, sc, h0, g, win, cw, cb, wx, bx, wa, ba, lam)


def _rope(xh, c, sneg, spos):
    half = ROT_DIM // 2
    return xh * c + pltpu.roll(xh, V7X_LANES - half, 1) * sneg + pltpu.roll(xh, half, 1) * spos


def _rope_tables(pos):
    half = ROT_DIM // 2
    inv = jnp.power(ROPE_THETA, -jnp.arange(half, dtype=F32) * 2.0 / ROT_DIM)
    ang = pos[:, None] * inv[None, :]
    cos, sin = jnp.cos(ang), jnp.sin(ang)
    n = pos.shape[0]
    ones = jnp.ones((n, HEAD_DIM - ROT_DIM), F32)
    zeros = jnp.zeros((n, HEAD_DIM - ROT_DIM), F32)
    zh = jnp.zeros((n, half), F32)
    c = jnp.concatenate([cos, cos, ones], axis=-1)
    sneg = jnp.concatenate([-sin, zh, zeros], axis=-1)
    spos = jnp.concatenate([zh, sin, zeros], axis=-1)
    rep = V7X_LANES // HEAD_DIM
    return tuple(jnp.tile(t, (1, rep)) for t in (c, sneg, spos))


def _qkv_sample_kernel(x_ref, g_ref, w_ref, cos_ref, sneg_ref, spos_ref, q_ref, k_ref, v_ref):
    n = _rms(x_ref[...], g_ref[...], NORM_EPS).astype(BF16)
    qkv = jnp.dot(n, w_ref[...], preferred_element_type=F32)
    tabs = (cos_ref[...], sneg_ref[...], spos_ref[...])
    for h in range(N_HEADS):
        lo, hi = h * V_DIM, (h + 1) * V_DIM
        q_ref[:, lo:hi] = _rope(qkv[:, lo:hi], *tabs) * QK_SCALE
        k_ref[:, lo:hi] = _rope(qkv[:, D_MODEL + lo:D_MODEL + hi], *tabs)
    v_ref[...] = qkv[:, 2 * D_MODEL:]


def _qkv_sample(x, g, w, cos, sneg, spos, *, layer):
    m = x.shape[0]
    out = jax.ShapeDtypeStruct((m, D_MODEL), F32)
    row_spec = _const_spec((m, D_MODEL))
    tab_spec = _const_spec((m, V7X_LANES))
    return pl.pallas_call(
        _qkv_sample_kernel,
        grid=(1,),
        in_specs=[row_spec, _const_spec((1, D_MODEL)), _layer_spec((D_MODEL, 3 * D_MODEL), layer),
                  tab_spec, tab_spec, tab_spec],
        out_specs=[pl.BlockSpec((m, D_MODEL), lambda i: (0, 0))] * 3,
        out_shape=[out, out, out],
        compiler_params=_cparams(("arbitrary",)),
        name="qkv_sample",
    )(x, g, w, cos, sneg, spos)


def _qkv_prompt_kernel(x_ref, g_ref, w_ref, cos_ref, sneg_ref, spos_ref, *rest):
    q_ref, kt_ref, vo_ref, ktb_ref, vb_ref = rest[-5:]
    n = _rms(x_ref[...], g_ref[...], NORM_EPS).astype(BF16)
    qkv = jnp.dot(n, w_ref[...], preferred_element_type=F32)
    tabs = (cos_ref[...], sneg_ref[...], spos_ref[...])
    for h in range(N_HEADS):
        lo, hi = h * V_DIM, (h + 1) * V_DIM
        q_ref[h] = (_rope(qkv[:, lo:hi], *tabs) * QK_SCALE).astype(q_ref.dtype)
        kt = _rope(qkv[:, D_MODEL + lo:D_MODEL + hi], *tabs).T
        kt_ref[lo:hi, :] = kt
        ktb_ref[lo:hi, :] = kt.astype(BF16)
        vh = qkv[:, 2 * D_MODEL + lo:2 * D_MODEL + hi]
        vo_ref[:, h, :] = vh
        vb_ref[h] = vh.astype(BF16)


def _qkv_prompt(x, g, w, cos, sneg, spos, kt_all, v_all, *, layer, batch, seq):
    tm = ROW_TILE
    nt = seq // tm
    row_spec = pl.BlockSpec((tm, D_MODEL), lambda b, t: (b * nt + t, 0))
    tab_spec = pl.BlockSpec((tm, V7X_LANES), lambda b, t: (t, 0))
    in_specs = [row_spec, _const_spec((1, D_MODEL)), _layer_spec((D_MODEL, 3 * D_MODEL), layer),
                tab_spec, tab_spec, tab_spec]
    args = [x, g, w, cos, sneg, spos]
    aliases = {}
    if kt_all is not None:
        in_specs += [pl.BlockSpec(memory_space=pl.ANY)] * 2
        aliases = {len(args): 1, len(args) + 1: 2}
        args += [kt_all, v_all]
    return pl.pallas_call(
        _qkv_prompt_kernel,
        grid=(batch, nt),
        in_specs=in_specs,
        out_specs=[pl.BlockSpec((None, N_HEADS, tm, V_DIM), lambda b, t: (b, 0, t, 0)),
                   pl.BlockSpec((None, None, D_MODEL, tm), lambda b, t: (layer, b, 0, t)),
                   pl.BlockSpec((None, None, tm, N_HEADS, V_DIM), lambda b, t: (layer, b, t, 0, 0)),
                   pl.BlockSpec((None, D_MODEL, tm), lambda b, t: (b, 0, t)),
                   pl.BlockSpec((None, N_HEADS, tm, V_DIM), lambda b, t: (b, 0, t, 0))],
        out_shape=[jax.ShapeDtypeStruct((batch, N_HEADS, seq, V_DIM), BF16),
                   jax.ShapeDtypeStruct((N_ATTN_LAYERS, batch, D_MODEL, seq), F32),
                   jax.ShapeDtypeStruct((N_ATTN_LAYERS, batch, seq, N_HEADS, V_DIM), F32),
                   jax.ShapeDtypeStruct((batch, D_MODEL, seq), BF16),
                   jax.ShapeDtypeStruct((batch, N_HEADS, seq, V_DIM), BF16)],
        input_output_aliases=aliases,
        compiler_params=_cparams(("arbitrary", "arbitrary")),
        name="qkv_prompt",
    )(*args)


def _diff_lambda(lq1, lk1, lq2, lk2, lam_init):
    s1 = jnp.sum(lq1 * lk1, axis=-1, keepdims=True)
    s2 = jnp.sum(lq2 * lk2, axis=-1, keepdims=True)
    return jnp.exp(s1) - jnp.exp(s2) + lam_init


def _attn_kernel(qi_ref, ki_ref, lq1_ref, lk1_ref, lq2_ref, lk2_ref, sg_ref, q_ref, kt_ref, v_ref, o_ref,
                 q2_ref, m_ref, acc_ref, oh_ref, *, tile, lam_init):
    step = pl.program_id(1)
    qi = qi_ref[step]
    ki = ki_ref[step]

    @pl.when(ki == 0)
    def _():
        q = q_ref[...]
        lane = lax.broadcasted_iota(jnp.int32, q.shape, 2)
        zero = jnp.zeros_like(q)
        q2_ref[:, 0:tile, :] = jnp.where(lane < HEAD_DIM, q, zero)
        q2_ref[:, tile:2 * tile, :] = jnp.where(lane >= HEAD_DIM, q, zero)

    def update(h, diagonal, first):
        kt = kt_ref[pl.ds(pl.multiple_of(h * V_DIM, V_DIM), V_DIM), :]
        s = jnp.dot(q2_ref[h], kt, preferred_element_type=F32)
        if diagonal:
            row = lax.broadcasted_iota(jnp.int32, s.shape, 0)
            col = lax.broadcasted_iota(jnp.int32, s.shape, 1)
            s = jnp.where(col <= jnp.where(row >= tile, row - tile, row), s, NEG_INF)
        m_cur = jnp.max(s, axis=-1, keepdims=True)
        v1 = jnp.concatenate([v_ref[h], jnp.ones((tile, V7X_LANES), BF16)], axis=1)
        if first:
            m_new = jnp.broadcast_to(m_cur, (2 * tile, V7X_LANES))
            p = jnp.exp2(s - jnp.tile(m_new, (1, tile // V7X_LANES))).astype(BF16)
            acc_ref[h] = jnp.dot(p, v1, preferred_element_type=F32)
        else:
            m_prev = m_ref[h]
            m_new = jnp.maximum(m_prev, m_cur)
            alpha = jnp.exp2(m_prev - m_new)
            p = jnp.exp2(s - jnp.tile(m_new, (1, tile // V7X_LANES))).astype(BF16)
            acc_ref[h] = jnp.tile(alpha, (1, 2)) * acc_ref[h] + jnp.dot(p, v1, preferred_element_type=F32)
        m_ref[h] = m_new

    def finish(h, lam):
        acc = acc_ref[h]
        o = acc[:, 0:V_DIM] / acc[:, V_DIM:2 * V_DIM]
        o = o[0:tile, :] - lam * o[tile:2 * tile, :]
        oh_ref[h] = (_rms(o, sg_ref[...], SUBLN_EPS) * (1.0 - lam_init)).astype(oh_ref.dtype)

    def heads(diagonal, first):
        lam = _diff_lambda(lq1_ref[...], lk1_ref[...], lq2_ref[...], lk2_ref[...], lam_init) if diagonal else None

        def head(h, carry):
            update(h, diagonal, first)
            if diagonal:
                finish(h, lam)
            return carry
        lax.fori_loop(0, N_HEADS, head, 0)
        if diagonal:
            for h in range(N_HEADS):
                o_ref[:, h * V_DIM:(h + 1) * V_DIM] = oh_ref[h]

    for first in (True, False):
        for diagonal in (True, False):
            cond = ((ki == 0) if first else (ki > 0)) & ((ki == qi) if diagonal else (ki < qi))
            pl.when(cond)(functools.partial(heads, diagonal, first))


def _attn_prompt(lq1, lk1, lq2, lk2, sg, q, ktb, vb, *, batch, seq, lam_init):
    tile = ATTN_TILE
    nq = seq // tile
    pairs = [(i, j) for i in range(nq) for j in range(i + 1)]
    qi_tab = jnp.array([i for i, _ in pairs], jnp.int32)
    ki_tab = jnp.array([j for _, j in pairs], jnp.int32)
    const2 = lambda shape: pl.BlockSpec(shape, lambda b, s, qt, kt: (0, 0))
    lam_spec = const2((1, HEAD_DIM))
    q_spec = pl.BlockSpec((None, N_HEADS, tile, V_DIM), lambda b, s, qt, kt: (b, 0, qt[s], 0))
    kt_spec = pl.BlockSpec((None, D_MODEL, tile), lambda b, s, qt, kt: (b, 0, kt[s]))
    v_spec = pl.BlockSpec((None, N_HEADS, tile, V_DIM), lambda b, s, qt, kt: (b, 0, kt[s], 0))
    grid_spec = pltpu.PrefetchScalarGridSpec(
        num_scalar_prefetch=2,
        grid=(batch, len(pairs)),
        in_specs=[lam_spec, lam_spec, lam_spec, lam_spec, const2((1, V_DIM)), q_spec, kt_spec, v_spec],
        out_specs=pl.BlockSpec((tile, D_MODEL), lambda b, s, qt, kt: (b * nq + qt[s], 0)),
        scratch_shapes=[pltpu.VMEM((N_HEADS, 2 * tile, V_DIM), BF16),
                        pltpu.VMEM((N_HEADS, 2 * tile, V7X_LANES), F32),
                        pltpu.VMEM((N_HEADS, 2 * tile, 2 * V_DIM), F32),
                        pltpu.VMEM((N_HEADS, tile, V_DIM), BF16)],
    )
    return pl.pallas_call(
        functools.partial(_attn_kernel, tile=tile, lam_init=lam_init),
        grid_spec=grid_spec,
        out_shape=jax.ShapeDtypeStruct((batch * seq, D_MODEL), BF16),
        compiler_params=_cparams(("arbitrary", "arbitrary")),
        name="attn_prompt",
    )(qi_tab, ki_tab, lq1, lk1, lq2, lk2, sg, q, ktb, vb)


def _attn_sample_kernel(pt_ref, lq1_ref, lk1_ref, lq2_ref, lk2_ref, sg_ref, e_ref, q_ref, kn_ref, vn_ref, *rest,
                        nt, lam_init):
    del pt_ref
    npp = PAGES_PER_STEP
    k_refs, v_refs = rest[:npp], rest[npp:2 * npp]
    o_ref = rest[2 * npp]
    qt_ref, qtb_ref, m_ref, l_ref, acc_ref = rest[2 * npp + 1:]
    j = pl.program_id(1)
    nrow = nt * N_HEADS

    @pl.when(j == 0)
    def _():
        q = q_ref[0]
        sub = lax.broadcasted_iota(jnp.int32, (N_HEADS, D_MODEL), 0)
        col = lax.broadcasted_iota(jnp.int32, (N_HEADS, D_MODEL), 1)
        for c in range(2):
            keep = (col // V_DIM == sub) & ((col // HEAD_DIM) % 2 == c)
            for t in range(nt):
                r0 = c * nrow + t * N_HEADS
                qt_ref[r0:r0 + N_HEADS, :] = jnp.where(keep, jnp.broadcast_to(q[t:t + 1, :], (N_HEADS, D_MODEL)), 0.0)
        qtb_ref[...] = qt_ref[...].astype(BF16)

        qt = qt_ref[...]
        kn = kn_ref[0]
        r = lax.broadcasted_iota(jnp.int32, (2 * nrow, 1), 0)
        t_row = (r // N_HEADS) % nt
        sn = []
        for t in range(nt):
            st = jnp.sum(qt * kn[t:t + 1, :], axis=-1, keepdims=True)
            sn.append(jnp.where(t <= t_row, st, NEG_INF))
        m = sn[0]
        for st in sn[1:]:
            m = jnp.maximum(m, st)
        l = jnp.zeros_like(m)
        acc = jnp.zeros(acc_ref.shape, F32)
        for t in range(nt):
            pt = jnp.exp2(sn[t] - m)
            l = l + pt
            vt = jnp.concatenate([vn_ref[0, t]] * (2 * nt), axis=0)
            acc = acc + pt * vt
        m_ref[...] = m
        l_ref[...] = l
        acc_ref[...] = acc

    qtb = qtb_ref[...]
    s = jnp.concatenate([jnp.dot(qtb, k_refs[i][...].astype(BF16), preferred_element_type=F32)
                         for i in range(npp)], axis=1)
    m_prev = m_ref[...]
    m_new = jnp.maximum(m_prev, jnp.max(s, axis=-1, keepdims=True))
    alpha = jnp.exp2(m_prev - m_new)
    p = jnp.exp2(s - m_new)
    l_ref[...] = alpha * l_ref[...] + jnp.sum(p, axis=-1, keepdims=True)
    m_ref[...] = m_new
    pb = p.astype(BF16)
    pstack = jnp.concatenate([pb[:, i * PAGE_SIZE:(i + 1) * PAGE_SIZE] for i in range(npp)], axis=0)
    pe = jnp.dot(pstack, e_ref[...], preferred_element_type=F32)
    hh = lax.broadcasted_iota(jnp.int32, (N_HEADS, pe.shape[1]), 0)
    cc = lax.broadcasted_iota(jnp.int32, (N_HEADS, pe.shape[1]), 1)
    own = (hh == cc % N_HEADS)[None]
    pe = jnp.where(own, pe.reshape(pe.shape[0] // N_HEADS, N_HEADS, pe.shape[1]), 0.0).reshape(pe.shape)
    pe = pe.astype(BF16)
    acc = alpha * acc_ref[...]
    for i in range(npp):
        vb = v_refs[i][...].astype(BF16)
        acc = acc + jnp.dot(pe[i * 2 * nrow:(i + 1) * 2 * nrow, :], vb, preferred_element_type=F32)
    acc_ref[...] = acc

    @pl.when(j == pl.num_programs(1) - 1)
    def _():
        acc = acc_ref[...] / l_ref[...]
        lam = _diff_lambda(lq1_ref[...], lk1_ref[...], lq2_ref[...], lk2_ref[...], lam_init)
        o = acc[0:nrow, :] - lam * acc[nrow:2 * nrow, :]
        o = _rms(o, sg_ref[...], SUBLN_EPS) * (1.0 - lam_init)
        for t in range(nt):
            o_ref[0, t] = o[t * N_HEADS:(t + 1) * N_HEADS, :]


def _attn_sample(page_table, lq1, lk1, lq2, lk2, sg, q, kn, vn, cache_k, cache_v, *, layer, lam_init):
    nb, nt, _ = q.shape
    n_layers, n_pool = cache_k.shape[:2]
    n_pages = page_table.shape[1]
    npp = PAGES_PER_STEP
    nj = n_pages // npp
    nrow2 = 2 * nt * N_HEADS
    vrows = PAGE_SIZE * N_HEADS

    def page_map(i):
        return lambda b, j, pt: (layer, pt[b, j * npp + i], 0, 0)

    const2 = lambda shape: pl.BlockSpec(shape, lambda b, j, pt: (0, 0))
    seq_spec = pl.BlockSpec((1, nt, D_MODEL), lambda b, j, pt: (b, 0, 0))
    head_spec = pl.BlockSpec((1, nt, N_HEADS, V_DIM), lambda b, j, pt: (b, 0, 0, 0))
    lam_spec = const2((1, HEAD_DIM))
    grid_spec = pltpu.PrefetchScalarGridSpec(
        num_scalar_prefetch=1,
        grid=(nb, nj),
        in_specs=[lam_spec, lam_spec, lam_spec, lam_spec, const2((1, V_DIM)), const2((PAGE_SIZE, vrows)),
                  seq_spec, seq_spec, head_spec]
                 + [pl.BlockSpec((None, None, D_MODEL, PAGE_SIZE), page_map(i)) for i in range(npp)]
                 + [pl.BlockSpec((None, None, vrows, V_DIM), page_map(i)) for i in range(npp)],
        out_specs=head_spec,
        scratch_shapes=[pltpu.VMEM((nrow2, D_MODEL), F32), pltpu.VMEM((nrow2, D_MODEL), BF16),
                        pltpu.VMEM((nrow2, 1), F32), pltpu.VMEM((nrow2, 1), F32),
                        pltpu.VMEM((nrow2, V_DIM), F32)],
    )
    ck = jnp.transpose(cache_k, (0, 1, 3, 4, 5, 2)).reshape(n_layers, n_pool, D_MODEL, PAGE_SIZE)
    cv = cache_v.reshape(n_layers, n_pool, vrows, V_DIM)
    spread = (jnp.arange(vrows)[None, :] // N_HEADS == jnp.arange(PAGE_SIZE)[:, None]).astype(BF16)
    return pl.pallas_call(
        functools.partial(_attn_sample_kernel, nt=nt, lam_init=lam_init),
        grid_spec=grid_spec,
        out_shape=jax.ShapeDtypeStruct((nb, nt, N_HEADS, V_DIM), F32),
        compiler_params=_cparams(("arbitrary", "arbitrary")),
        name="attn_sample",
    )(page_table, lq1, lk1, lq2, lk2, sg, spread, q, kn, vn, *([ck] * npp), *([cv] * npp))


def _post_kernel(x_ref, y_ref, wp_ref, g_ref, wi_ref, wo_ref, gf_ref, o_ref, *, final):
    x = x_ref[...] + jnp.dot(y_ref[...].astype(BF16), wp_ref[...], preferred_element_type=F32)
    n = _rms(x, g_ref[...], NORM_EPS).astype(BF16)
    out = x
    for c in range(N_FF_CHUNKS):
        lo, hi = c * FF_CHUNK, (c + 1) * FF_CHUNK
        gc = jnp.dot(n, wi_ref[:, lo:hi], preferred_element_type=F32)
        uc = jnp.dot(n, wi_ref[:, D_FF + lo:D_FF + hi], preferred_element_type=F32)
        act = (gc * jax.nn.sigmoid(gc) * uc).astype(BF16)
        out = out + jnp.dot(act, wo_ref[lo:hi, :], preferred_element_type=F32)
    if final:
        out = _rms(out, gf_ref[...], NORM_EPS)
    o_ref[...] = out


def _post(x, y, wp, g, wi, wo, gf, *, tm, final, proj_layer, ffn_layer):
    m = x.shape[0]
    row_spec = pl.BlockSpec((tm, D_MODEL), lambda i: (i, 0))
    return pl.pallas_call(
        functools.partial(_post_kernel, final=final),
        grid=(m // tm,),
        in_specs=[row_spec, row_spec, _layer_spec((D_MODEL, D_MODEL), proj_layer), _const_spec((1, D_MODEL)),
                  _layer_spec((D_MODEL, 2 * D_FF), ffn_layer), _layer_spec((D_FF, D_MODEL), ffn_layer),
                  _const_spec((1, D_MODEL))],
        out_specs=row_spec,
        out_shape=jax.ShapeDtypeStruct((m, D_MODEL), F32),
        compiler_params=_cparams(("arbitrary",)),
        name="post",
    )(x, y, wp, g, wi, wo, gf)


def kernel(x_prompt, x_sample, cache_k, cache_v, page_table, state_conv, state_h, norm_mix, w_rec_in, conv_w,
           conv_b, gate_x_w, gate_x_b, gate_a_w, gate_a_b, lru_lambda, w_rec_out, w_qkv, lambda_q1, lambda_k1,
           lambda_q2, lambda_k2, subln_g, w_attn_out, norm_ffn, w_ffn_in, w_ffn_out, norm_final):
    B, T, _ = x_prompt.shape
    DB, Tn, _ = x_sample.shape
    past = page_table.shape[1] * PAGE_SIZE
    row = lambda a: a.reshape(1, -1)

    hp = x_prompt.reshape(B * T, D_MODEL)
    hs = x_sample.reshape(DB * Tn, D_MODEL)
    tabs_p = _rope_tables(jnp.arange(T, dtype=F32))
    tabs_s = _rope_tables(jnp.tile(past + jnp.arange(Tn, dtype=F32), DB))

    w_rec_in_b, gate_x_b16, gate_a_b16 = w_rec_in.astype(BF16), gate_x_w.astype(BF16), gate_a_w.astype(BF16)
    w_rec_out_b, w_qkv_b, w_attn_out_b = w_rec_out.astype(BF16), w_qkv.astype(BF16), w_attn_out.astype(BF16)
    w_ffn_in_b, w_ffn_out_b = w_ffn_in.astype(BF16), w_ffn_out.astype(BF16)

    kt_all = v_all = None
    pconv, ph = [], []
    sk, sv, sconv, sh = [], [], [], []
    for i in range(DEPTH):
        g_mix = row(norm_mix[i])
        if i % N_MIXERS == 0:
            r = i // N_MIXERS
            w = (w_rec_in_b, conv_w[r], row(conv_b[r]), gate_x_b16, row(gate_x_b[r]), gate_a_b16, row(gate_a_b[r]),
                 row(lru_lambda[r]))
            yp, c_p, h_p = _rec_prompt(hp, g_mix, *w, layer=r, batch=B, seq=T)
            to_tm = lambda t, n: t.reshape(DB, n, D_RNN).swapaxes(0, 1).reshape(n * DB, D_RNN)
            from_tm = lambda t, n: t.reshape(n, DB, D_RNN).swapaxes(0, 1)
            ys, c_s, h_s = _rec_sample(to_tm(hs, Tn), to_tm(state_conv[r], CONV_W - 1), state_h[r], g_mix, *w,
                                       layer=r, nb=DB, nt=Tn)
            ys = from_tm(ys, Tn).reshape(DB * Tn, D_RNN)
            pconv.append(c_p); ph.append(h_p.reshape(B, D_RNN))
            sconv.append(from_tm(c_s, CONV_W - 1)); sh.append(h_s)
            w_proj, proj_layer = w_rec_out_b, r
        else:
            a = i // N_MIXERS
            lam_init = 0.8 - 0.6 * math.exp(-0.3 * i)
            lams = (row(lambda_q1[a]), row(lambda_k1[a]), row(lambda_q2[a]), row(lambda_k2[a]))
            q_p, kt_all, v_all, ktb, vb = _qkv_prompt(hp, g_mix, w_qkv_b, *tabs_p, kt_all, v_all,
                                                      layer=a, batch=B, seq=T)
            yp = _attn_prompt(*lams, row(subln_g[a]), q_p, ktb, vb, batch=B, seq=T, lam_init=lam_init)
            q_s, k_s, v_s = _qkv_sample(hs, g_mix, w_qkv_b, *tabs_s, layer=a)
            seq3 = lambda t: t.reshape(DB, Tn, D_MODEL)
            ys = _attn_sample(page_table, *lams, row(subln_g[a]), seq3(q_s), seq3(k_s),
                              v_s.reshape(DB, Tn, N_HEADS, V_DIM), cache_k, cache_v,
                              layer=a, lam_init=lam_init).reshape(DB * Tn, D_MODEL)
            sk.append(k_s.reshape(DB, Tn, N_HEADS, 2, HEAD_DIM)); sv.append(v_s.reshape(DB, Tn, N_HEADS, V_DIM))
            w_proj, proj_layer = w_attn_out_b, a

        post_w = (w_proj, row(norm_ffn[i]), w_ffn_in_b, w_ffn_out_b, row(norm_final))
        post_kw = dict(final=i == DEPTH - 1, proj_layer=proj_layer, ffn_layer=i)
        hp = _post(hp, yp, *post_w, tm=ROW_TILE, **post_kw)
        hs = _post(hs, ys, *post_w, tm=DB * Tn, **post_kw)

    prompt_k = jnp.transpose(kt_all.reshape(N_ATTN_LAYERS, B, N_HEADS, 2, HEAD_DIM, T), (0, 1, 5, 2, 3, 4))
    return (hp.reshape(B, T, D_MODEL), hs.reshape(DB, Tn, D_MODEL),
            prompt_k, v_all, jnp.stack(pconv), jnp.stack(ph),
            jnp.stack(sk), jnp.stack(sv), jnp.stack(sconv), jnp.stack(sh))
```
